```python
import jax, jax.numpy as jnp
from jax import lax
import numpy as np

D_MODEL = 1024
BATCH = 4
SEQ = 4096
DEPTH = 1
DEC_BATCH = 128
DEC_SEQ = 4
PAST_LEN = 8192
PAGE_SIZE = 128

HEAD_DIM = 64
N_ATT_HEADS = (D_MODEL // HEAD_DIM) // 2
N_KV_HEADS = N_ATT_HEADS
ATT_WIDTH = N_ATT_HEADS * HEAD_DIM
KV_WIDTH = N_KV_HEADS * HEAD_DIM
CONV_CH = D_MODEL - ATT_WIDTH
CONV_WIDTH = 31
IDX_HEADS = 8
IDX_DIM = 64
TOPK_MAX = 256
Q_BLOCK = 128
N_GROUPS = 4
EXPERTS_PER_GROUP = 8
N_EXPERTS = N_GROUPS * EXPERTS_PER_GROUP
TOP_K_EXPERTS = 2
D_FF_EXPERT = 256
EPS = 1e-6

OFF_Q = 0
OFF_K = OFF_Q + ATT_WIDTH
OFF_V = OFF_K + KV_WIDTH
OFF_QI = OFF_V + KV_WIDTH
OFF_KI = OFF_QI + IDX_HEADS * IDX_DIM
OFF_WI = OFF_KI + IDX_DIM
OFF_CONV = OFF_WI + IDX_HEADS
N_IN = OFF_CONV + 2 * CONV_CH

kernel_name = "hymba_dsa_conformer_hmoe_step"


def rmsnorm(x, g):
    xf = x.astype(jnp.float32)
    y = xf * lax.rsqrt(jnp.mean(xf * xf, axis=-1, keepdims=True) + EPS)
    return (y * g.astype(jnp.float32)).astype(x.dtype)


def layernorm(x, g, b):
    xf = x.astype(jnp.float32)
    mu = jnp.mean(xf, axis=-1, keepdims=True)
    xc = xf - mu
    y = xc * lax.rsqrt(jnp.mean(xc * xc, axis=-1, keepdims=True) + EPS)
    return (y * g.astype(jnp.float32) + b.astype(jnp.float32)).astype(x.dtype)


gather_rows = jax.vmap(lambda rows, ii: rows[ii])


def indexer_scores(qi, ki, wi):
    rel = jax.nn.relu(jnp.einsum('bthd,bld->bthl', qi.astype(jnp.float32), ki.astype(jnp.float32)))
    return jnp.einsum('bthl,bth->btl', rel, wi.astype(jnp.float32))


def sparse_attend(q, kg, vg, valid):
    s = jnp.einsum('bthd,btkhd->bthk', q, kg).astype(jnp.float32) * (HEAD_DIM ** -0.5)
    s = jnp.where(valid[:, :, None, :], s, -jnp.inf)
    p = jax.nn.softmax(s, axis=-1).astype(vg.dtype)
    return jnp.einsum('bthk,btkhd->bthd', p, vg)


def prompt_sparse_attention(q, k, v, qi, ki, wi):
    B, S = q.shape[:2]
    topk = min(TOPK_MAX, S // 4)
    nblk = S // Q_BLOCK
    key_pos = jnp.arange(S)

    def to_blocks(t):
        return t.reshape((B, nblk, Q_BLOCK) + t.shape[2:]).swapaxes(0, 1)

    def block(args):
        qb, qib, wib, start = args
        qpos = start + jnp.arange(Q_BLOCK)
        scores = indexer_scores(qib, ki, wib)
        scores = jnp.where(key_pos[None, None, :] <= qpos[None, :, None], scores, -jnp.inf)
        _, idx = lax.top_k(scores, topk)
        valid = idx <= qpos[None, :, None]
        return sparse_attend(qb, gather_rows(k, idx), gather_rows(v, idx), valid)

    starts = jnp.arange(nblk, dtype=jnp.int32) * Q_BLOCK
    out = lax.map(block, (to_blocks(q), to_blocks(qi), to_blocks(wi), starts))
    return out.swapaxes(0, 1).reshape(B, S, N_ATT_HEADS, HEAD_DIM)


def sample_sparse_attention(q, k_new, v_new, qi, ki_new, wi, cache_k, cache_v, cache_kidx, page_table):
    B, T = q.shape[:2]
    page = cache_k.shape[1]
    P = page_table.shape[1] * page
    L = P + T
    topk = min(TOPK_MAX, L // 4)
    ki_past = cache_kidx[page_table].reshape(B, P, IDX_DIM)
    ki_all = jnp.concatenate([ki_past, ki_new.astype(ki_past.dtype)], axis=1)
    qpos = P + jnp.arange(T)
    scores = indexer_scores(qi, ki_all, wi)
    scores = jnp.where(jnp.arange(L)[None, None, :] <= qpos[None, :, None], scores, -jnp.inf)
    _, idx = lax.top_k(scores, topk)
    valid = idx <= qpos[None, :, None]
    in_past = (idx < P)[..., None, None]
    pidx = jnp.minimum(idx, P - 1)
    phys = gather_rows(page_table, pidx // page)
    off = pidx % page
    nidx = jnp.clip(idx - P, 0, T - 1)
    kg = jnp.where(in_past, cache_k[phys, off], gather_rows(k_new, nidx).astype(cache_k.dtype))
    vg = jnp.where(in_past, cache_v[phys, off], gather_rows(v_new, nidx).astype(cache_v.dtype))
    return sparse_attend(q, kg.astype(q.dtype), vg.astype(q.dtype), valid)


def causal_dwconv(buf, w, b):
    out = lax.conv_general_dilated(buf, w.reshape(CONV_WIDTH, 1, CONV_CH).astype(buf.dtype), (1,), 'VALID',
                                   dimension_numbers=('NWC', 'WIO', 'NWC'), feature_group_count=CONV_CH)
    return out + b


def hier_moe(h, w_rg, b_rg, w_re, b_re, w1, w3, w2):
    B, T, D = h.shape
    flat = h.reshape(B * T, D)
    g_logits = (flat @ w_rg + b_rg).astype(jnp.float32)
    g_prob = jax.nn.softmax(g_logits, axis=-1)
    g_sel = jnp.argmax(g_logits, axis=-1)
    p_g = jnp.take_along_axis(g_prob, g_sel[:, None], axis=-1)
    e_logits = (jnp.einsum('nd,gde->nge', flat, w_re) + b_re).astype(jnp.float32)
    e_logits = jnp.take_along_axis(e_logits, g_sel[:, None, None], axis=1)[:, 0]
    e_val, e_idx = lax.top_k(e_logits, TOP_K_EXPERTS)
    p_e = jax.nn.softmax(e_val, axis=-1) * p_g
    expert_id = g_sel[:, None] * EXPERTS_PER_GROUP + e_idx
    gates = jnp.einsum('nk,nke->ne', p_e, jax.nn.one_hot(expert_id, N_EXPERTS, dtype=jnp.float32)).astype(h.dtype)
    out = jnp.zeros_like(flat)
    for e in range(N_EXPERTS):
        hid = jax.nn.silu(flat @ w1[e]) * (flat @ w3[e])
        out = out + gates[:, e:e + 1] * (hid @ w2[e])
    return out.reshape(B, T, D)


def layer(x, c, attn_fn, conv_prefix, w_ada, b_ada, g_norm1, w_in, conv_w, conv_b, conv_ln_g, conv_ln_b,
          g_attn_out, g_conv_out, w_out, g_norm2, w_rg, b_rg, w_re, b_re, w1, w3, w2):
    B, T, _ = x.shape
    mod = (jax.nn.silu(c) @ w_ada + b_ada).reshape(B, 6, 1, D_MODEL)
    shift1, scale1, gate1, shift2, scale2, gate2 = [mod[:, i] for i in range(6)]
    h = rmsnorm(x, g_norm1) * (1 + scale1) + shift1
    p = h @ w_in
    q = p[..., OFF_Q:OFF_K].reshape(B, T, N_ATT_HEADS, HEAD_DIM)
    k = p[..., OFF_K:OFF_V].reshape(B, T, N_KV_HEADS, HEAD_DIM)
    v = p[..., OFF_V:OFF_QI].reshape(B, T, N_KV_HEADS, HEAD_DIM)
    qi = p[..., OFF_QI:OFF_KI].reshape(B, T, IDX_HEADS, IDX_DIM)
    ki = p[..., OFF_KI:OFF_WI]
    wi = p[..., OFF_WI:OFF_CONV]
    glu = p[..., OFF_CONV:OFF_CONV + CONV_CH] * jax.nn.sigmoid(p[..., OFF_CONV + CONV_CH:N_IN])
    att = attn_fn(q, k, v, qi, ki, wi).reshape(B, T, ATT_WIDTH)
    buf = jnp.concatenate([conv_prefix.astype(glu.dtype), glu], axis=1)
    conv = jax.nn.silu(layernorm(causal_dwconv(buf, conv_w, conv_b), conv_ln_g, conv_ln_b))
    mixed = jnp.concatenate([rmsnorm(att, g_attn_out), rmsnorm(conv, g_conv_out)], axis=-1) @ w_out
    x = x + gate1 * mixed
    h2 = rmsnorm(x, g_norm2) * (1 + scale2) + shift2
    x = x + gate2 * hier_moe(h2, w_rg, b_rg, w_re, b_re, w1, w3, w2)
    return x, k, v, ki, buf[:, -(CONV_WIDTH - 1):]


def setup_inputs(seed: int = 0) -> dict:
    key = jax.random.key(seed)
    ks = jax.random.split(key, 32)
    f32 = jnp.float32
    nrm = lambda k, shape, s: jax.random.normal(k, shape, f32) * s
    n_pages = PAST_LEN // PAGE_SIZE
    n_pool = (DEC_BATCH * n_pages * 5) // 4
    page_table = jax.random.permutation(ks[9], n_pool)[:DEC_BATCH * n_pages].reshape(DEC_BATCH, n_pages).astype(jnp.int32)
    return {
        'x_prompt': nrm(ks[0], (BATCH, SEQ, D_MODEL), 1.0),
        'x_sample': nrm(ks[1], (DEC_BATCH, DEC_SEQ, D_MODEL), 1.0),
        'c_prompt': nrm(ks[2], (BATCH, D_MODEL), 1.0),
        'c_sample': nrm(ks[3], (DEC_BATCH, D_MODEL), 1.0),
        'cache_k': nrm(ks[4], (n_pool, PAGE_SIZE, N_KV_HEADS, HEAD_DIM), 1.0),
        'cache_v': nrm(ks[5], (n_pool, PAGE_SIZE, N_KV_HEADS, HEAD_DIM), 1.0),
        'cache_kidx': nrm(ks[6], (n_pool, PAGE_SIZE, IDX_DIM), 1.0),
        'state_conv': nrm(ks[7], (DEC_BATCH, CONV_WIDTH - 1, CONV_CH), 0.5),
        'page_table': page_table,
        'w_ada': nrm(ks[10], (D_MODEL, 6 * D_MODEL), 0.5 * D_MODEL ** -0.5),
        'b_ada': nrm(ks[11], (6 * D_MODEL,), 0.02),
        'g_norm1': 1.0 + nrm(ks[12], (D_MODEL,), 0.02),
        'w_in': nrm(ks[13], (D_MODEL, N_IN), D_MODEL ** -0.5),
        'conv_w': nrm(ks[14], (CONV_WIDTH, CONV_CH), CONV_WIDTH ** -0.5),
        'conv_b': nrm(ks[15], (CONV_CH,), 0.02),
        'conv_ln_g': 1.0 + nrm(ks[16], (CONV_CH,), 0.02),
        'conv_ln_b': nrm(ks[17], (CONV_CH,), 0.02),
        'g_attn_out': 1.0 + nrm(ks[18], (ATT_WIDTH,), 0.02),
        'g_conv_out': 1.0 + nrm(ks[19], (CONV_CH,), 0.02),
        'w_out': nrm(ks[20], (D_MODEL, D_MODEL), D_MODEL ** -0.5),
        'g_norm2': 1.0 + nrm(ks[21], (D_MODEL,), 0.02),
        'w_rg': nrm(ks[22], (D_MODEL, N_GROUPS), D_MODEL ** -0.5),
        'b_rg': nrm(ks[23], (N_GROUPS,), 0.01),
        'w_re': nrm(ks[24], (N_GROUPS, D_MODEL, EXPERTS_PER_GROUP), D_MODEL ** -0.5),
        'b_re': nrm(ks[25], (N_GROUPS, EXPERTS_PER_GROUP), 0.01),
        'w1': nrm(ks[26], (N_EXPERTS, D_MODEL, D_FF_EXPERT), D_MODEL ** -0.5),
        'w3': nrm(ks[27], (N_EXPERTS, D_MODEL, D_FF_EXPERT), D_MODEL ** -0.5),
        'w2': nrm(ks[28], (N_EXPERTS, D_FF_EXPERT, D_MODEL), D_FF_EXPERT ** -0.5),
        'g_final': 1.0 + nrm(ks[29], (D_MODEL,), 0.02),
    }


def reference(x_prompt, x_sample, c_prompt, c_sample, cache_k, cache_v, cache_kidx, state_conv, page_table,
              w_ada, b_ada, g_norm1, w_in, conv_w, conv_b, conv_ln_g, conv_ln_b, g_attn_out, g_conv_out,
              w_out, g_norm2, w_rg, b_rg, w_re, b_re, w1, w3, w2, g_final):
    weights = (w_ada, b_ada, g_norm1, w_in, conv_w, conv_b, conv_ln_g, conv_ln_b, g_attn_out, g_conv_out,
               w_out, g_norm2, w_rg, b_rg, w_re, b_re, w1, w3, w2)
    sample_attn = lambda q, k, v, qi, ki, wi: sample_sparse_attention(q, k, v, qi, ki, wi, cache_k, cache_v,
                                                                      cache_kidx, page_table)
    hp = x_prompt
    hs = x_sample
    prompt_prefix = jnp.zeros((x_prompt.shape[0], CONV_WIDTH - 1, CONV_CH), x_prompt.dtype)
    for _ in range(DEPTH):
        hp, k_p, v_p, ki_p, conv_p = layer(hp, c_prompt, prompt_sparse_attention, prompt_prefix, *weights)
        hs, k_s, v_s, ki_s, conv_s = layer(hs, c_sample, sample_attn, state_conv, *weights)
    y_prompt = rmsnorm(hp, g_final)
    y_sample = rmsnorm(hs, g_final)
    return (y_prompt, y_sample, k_p, v_p, ki_p, conv_p, k_s, v_s, ki_s, conv_s)
```

```python
import functools

import jax
import jax.numpy as jnp
from jax import lax
from jax.experimental import pallas as pl
from jax.experimental.pallas import tpu as pltpu

F32 = jnp.float32
BF16 = jnp.bfloat16
I32 = jnp.int32

EPS = 1e-6
HEAD_DIM = 64
N_HEADS = 8
ATT_WIDTH = N_HEADS * HEAD_DIM
IDX_HEADS = 8
IDX_DIM = 64
CONV_WIDTH = 31
TOPK_MAX = 256
N_GROUPS = 4
EXPERTS_PER_GROUP = 8
N_EXPERTS = N_GROUPS * EXPERTS_PER_GROUP
LANES = 128
HALO = 32
PAGES_PER_STEP = 16
NEG_BIG = -1e30
MAX_BISECT = 40
KEEP_ALL = 1e9
VMEM_LIMIT = 56 * 1024 * 1024


def _cparams(sem):
    return pltpu.CompilerParams(dimension_semantics=sem, vmem_limit_bytes=VMEM_LIMIT)


def _rms(x):
    return x * lax.rsqrt(jnp.mean(x * x, axis=-1, keepdims=True) + EPS)


def _dot(a, b):
    return jnp.dot(a, b, preferred_element_type=F32)


def _dot_nt(a, b):
    return lax.dot_general(a, b, (((1,), (1,)), ((), ())), preferred_element_type=F32)


def _mod_kernel(c_ref, w_ref, b_ref, o_ref):
    c = c_ref[...]
    a = (c * jax.nn.sigmoid(c)).astype(BF16)
    o_ref[...] = _dot(a, w_ref[...].astype(BF16)) + b_ref[...]


def _modulation(c, w_ada, b_ada, tn=512):
    r, d = c.shape
    n = w_ada.shape[1]
    return pl.pallas_call(
        _mod_kernel,
        grid=(n // tn,),
        in_specs=[pl.BlockSpec((r, d), lambda j: (0, 0)),
                  pl.BlockSpec((d, tn), lambda j: (0, j)),
                  pl.BlockSpec((1, tn), lambda j: (0, j))],
        out_specs=pl.BlockSpec((r, tn), lambda j: (0, j)),
        out_shape=jax.ShapeDtypeStruct((r, n), F32),
        compiler_params=_cparams(("arbitrary",)),
        name="adaln_mod",
    )(c, w_ada, b_ada.reshape(1, n))


def _inproj_kernel(x_ref, sh_ref, sc_ref, g_ref, wqkv_ref, wqi_ref, wkw_ref, wca_ref, wcb_ref,
                   q_ref, k_ref, v_ref, kb_ref, vb_ref, qi_ref, kw_ref, kib_ref, glu_ref):
    h = _rms(x_ref[...]) * g_ref[...]
    h = h * (1.0 + sc_ref[0]) + sh_ref[0]
    hb = h.astype(BF16)
    qkv = _dot(hb, wqkv_ref[...])
    q_ref[...] = (qkv[:, :ATT_WIDTH] * (HEAD_DIM ** -0.5)).astype(BF16)
    k = qkv[:, ATT_WIDTH:2 * ATT_WIDTH]
    v = qkv[:, 2 * ATT_WIDTH:]
    k_ref[...] = k
    v_ref[...] = v
    kb_ref[...] = k.astype(BF16)
    vb_ref[...] = v.astype(BF16)
    qi_ref[...] = _dot(hb, wqi_ref[...]).astype(BF16)
    kw = _dot(hb, wkw_ref[...])
    kw_ref[...] = kw
    kib_ref[...] = kw[:, :IDX_DIM].astype(BF16)
    a = _dot(hb, wca_ref[...])
    b = _dot(hb, wcb_ref[...])
    glu_ref[...] = a * jax.nn.sigmoid(b)


def _mod_spec(rows, d, tiles_per_group):
    return pl.BlockSpec((1, rows, d), lambda i: (i // tiles_per_group, 0, 0))


def _inproj(x, shift, scale, g1, wts, tm, tiles_per_group):
    n, d = x.shape
    wqkv, wqi, wkw, wca, wcb = wts
    cc = wca.shape[1]
    rows = shift.shape[1]
    full = lambda a: pl.BlockSpec(a.shape, lambda i: (0,) * a.ndim)
    row = lambda w: pl.BlockSpec((tm, w), lambda i: (i, 0))
    outs = [
        (ATT_WIDTH, BF16),
        (ATT_WIDTH, F32),
        (ATT_WIDTH, F32),
        (ATT_WIDTH, BF16),
        (ATT_WIDTH, BF16),
        (IDX_HEADS * IDX_DIM, BF16),
        (LANES, F32),
        (IDX_DIM, BF16),
        (cc, F32),
    ]
    return pl.pallas_call(
        _inproj_kernel,
        grid=(n // tm,),
        in_specs=[row(d), _mod_spec(rows, d, tiles_per_group), _mod_spec(rows, d, tiles_per_group),
                  full(g1), full(wqkv), full(wqi), full(wkw), full(wca), full(wcb)],
        out_specs=[row(w) for w, _ in outs],
        out_shape=[jax.ShapeDtypeStruct((n, w), dt) for w, dt in outs],
        compiler_params=_cparams(("parallel",)),
        name="inproj",
    )(x, shift, scale, g1, wqkv, wqi, wkw, wca, wcb)


def _fold(scan, rows, init, fn, red):
    def body(sc, part):
        for g in range(sc.shape[1] // LANES):
            part = fn(part, sc[:, g * LANES:(g + 1) * LANES])
        return part
    return red(scan(body, jnp.full((rows, LANES), init, F32)), axis=-1, keepdims=True)


def _kth_select(scan, rows, k):
    inf = jnp.inf
    count_ge = lambda t: _fold(scan, rows, 0.0, lambda p, x: p + jnp.where(x >= t, 1.0, 0.0), jnp.sum)
    count_gt = lambda t: _fold(scan, rows, 0.0, lambda p, x: p + jnp.where(x > t, 1.0, 0.0), jnp.sum)
    min_ge = lambda t: _fold(scan, rows, inf, lambda p, x: jnp.minimum(p, jnp.where(x >= t, x, inf)), jnp.min)
    min_gt = lambda t: _fold(scan, rows, inf, lambda p, x: jnp.minimum(p, jnp.where(x > t, x, inf)), jnp.min)
    row_max = _fold(scan, rows, -inf, jnp.maximum, jnp.max)
    row_min = min_gt(jnp.full((rows, 1), -inf, F32))
    n_valid = count_ge(row_min)
    any_row = lambda flag: jnp.max(flag) > 0.5

    def bisect(lo, hi, cnt, pending):
        def cond(st):
            it, _, _, _, act = st
            return jnp.logical_and(it < MAX_BISECT, any_row(act))

        def body(st):
            it, lo, hi, cnt, act = st
            mid = 0.5 * (lo + hi)
            c = count_ge(mid)
            go_up = jnp.where(c >= k, act, 0.0) > 0.5
            go_dn = jnp.where(c >= k, 0.0, act) > 0.5
            lo2 = jnp.where(go_up, mid, lo)
            cnt2 = jnp.where(go_up, c, cnt)
            hi2 = jnp.where(go_dn, mid, hi)
            moving = jnp.logical_and(jnp.logical_and(mid > lo, mid < hi), cnt2 > k)
            return it + 1, lo2, hi2, cnt2, jnp.where(moving, act, 0.0)

        act0 = jnp.where(cnt > k, pending, 0.0)
        _, lo, hi, cnt, _ = lax.while_loop(cond, body, (jnp.int32(0), lo, hi, cnt, act0))
        return lo, hi, cnt

    def outer_cond(st):
        return any_row(st[5])

    def outer_body(st):
        lo, hi, cnt, thr, need, pending = st
        lo, hi, cnt = bisect(lo, hi, cnt, pending)
        plain = jnp.where(cnt <= k, pending, 0.0)
        hard = jnp.where(cnt <= k, 0.0, pending)
        thr = jnp.where(plain > 0.5, lo, thr)

        def resolve(_):
            t = min_ge(lo)
            n_gt = count_gt(t)
            tied = jnp.where(n_gt < k, hard, 0.0) > 0.5
            above_f = jnp.where(n_gt < k, 0.0, hard)
            above = above_f > 0.5
            return (jnp.where(above, min_gt(t), lo), jnp.where(above, n_gt, cnt),
                    jnp.where(tied, t, thr), jnp.where(tied, k - n_gt, need), above_f)

        def settled(_):
            return lo, cnt, thr, need, jnp.zeros((rows, 1), F32)

        lo, cnt, thr, need, pending = lax.cond(any_row(hard), resolve, settled, 0)
        return lo, hi, cnt, thr, need, pending

    lo0 = row_min
    hi0 = row_max + (row_max - row_min) + 1.0
    st0 = (lo0, hi0, n_valid, lo0, jnp.full((rows, 1), KEEP_ALL, F32), jnp.ones((rows, 1), F32))
    _, _, _, thr, need, _ = lax.while_loop(outer_cond, outer_body, st0)
    return thr, need


def _prompt_attn_kernel(q_ref, qi_ref, kw_ref, kb_ref, vb_ref, kib_ref, o_ref,
                        score_ref, bias_ref, *, tq, tk, topk):
    i = pl.program_id(1)
    nk = ((i + 1) * tq + tk - 1) // tk
    lane = lax.broadcasted_iota(I32, (tq, tk), 1)
    qpos = i * tq + lax.broadcasted_iota(I32, (tq, tk), 0)
    wi = kw_ref[0][:, IDX_DIM:IDX_DIM + IDX_HEADS]
    qi = qi_ref[0]

    def score_chunk(c, carry):
        kic = kib_ref[0, pl.ds(pl.multiple_of(c * tk, tk), tk), :]
        acc = jnp.zeros((tq, tk), F32)
        for h in range(IDX_HEADS):
            s = _dot_nt(qi[:, h * IDX_DIM:(h + 1) * IDX_DIM], kic)
            acc = acc + wi[:, h:h + 1] * jnp.maximum(s, 0.0)
        valid = (c * tk + lane) <= qpos
        score_ref[c] = jnp.where(valid, acc, -jnp.inf)
        return carry
    lax.fori_loop(0, nk, score_chunk, 0)

    def scan(body, init):
        return lax.fori_loop(0, nk, lambda c, part: body(score_ref[c], part), init)
    thr, need = _kth_select(scan, tq, topk)

    def bias_chunk(c, carry):
        bias_ref[c] = jnp.where(score_ref[c] >= thr, 0.0, NEG_BIG)
        return carry
    lax.fori_loop(0, nk, bias_chunk, 0)

    tie_f = jnp.where(need < KEEP_ALL, 1.0, 0.0)

    @pl.when(jnp.max(tie_f) > 0.5)
    def _():
        upper = (lax.broadcasted_iota(I32, (tk, tk), 0)
                 < lax.broadcasted_iota(I32, (tk, tk), 1)).astype(BF16)

        def tie_chunk(c, before):
            sc = score_ref[c]
            eq = sc == thr
            eqf = jnp.where(eq, 1.0, 0.0)
            rank = before + _dot(eqf.astype(BF16), upper)
            tie_bias = jnp.where(jnp.logical_or(sc > thr, jnp.logical_and(eq, rank < need)), 0.0, NEG_BIG)
            bias_ref[c] = jnp.where(tie_f > 0.5, tie_bias, bias_ref[c])
            return before + jnp.sum(eqf, axis=-1, keepdims=True)
        lax.fori_loop(0, nk, tie_chunk, jnp.zeros((tq, 1), F32))

    half = lax.broadcasted_iota(I32, (tq, LANES), 1) < HEAD_DIM
    for p in range(N_HEADS // 2):
        qp = q_ref[0, :, p * LANES:(p + 1) * LANES]
        qs = (jnp.where(half, qp, jnp.zeros_like(qp)), jnp.where(half, jnp.zeros_like(qp), qp))

        def attn_chunk(c, carry):
            off = pl.multiple_of(c * tk, tk)
            kc = kb_ref[0, pl.ds(off, tk), p * LANES:(p + 1) * LANES]
            vc = vb_ref[0, pl.ds(off, tk), p * LANES:(p + 1) * LANES]
            bias = bias_ref[c]
            new = []
            for e in range(2):
                m, l, acc = carry[e]
                s = _dot_nt(qs[e], kc) + bias
                m_new = jnp.maximum(m, jnp.max(s, axis=-1, keepdims=True))
                alpha = jnp.exp(m - m_new)
                pr = jnp.exp(s - m_new)
                l = alpha * l + jnp.sum(pr, axis=-1, keepdims=True)
                acc = alpha * acc + _dot(pr.astype(BF16), vc)
                new.append((m_new, l, acc))
            return tuple(new)

        init = tuple((jnp.full((tq, 1), NEG_BIG, F32), jnp.zeros((tq, 1), F32),
                      jnp.zeros((tq, LANES), F32)) for _ in range(2))
        (_, l0, a0), (_, l1, a1) = lax.fori_loop(0, nk, attn_chunk, init)
        o_ref[0, :, p * LANES:(p + 1) * LANES] = jnp.where(half, a0 / l0, a1 / l1)


def _prompt_attention(q, qi, kw, kb, vb, kib, tq, tk):
    b, s, _ = q.shape
    topk = min(TOPK_MAX, s // 4)
    blk = lambda w: pl.BlockSpec((1, tq, w), lambda bi, i: (bi, i, 0))
    seq = lambda w: pl.BlockSpec((1, s, w), lambda bi, i: (bi, 0, 0))
    return pl.pallas_call(
        functools.partial(_prompt_attn_kernel, tq=tq, tk=tk, topk=topk),
        grid=(b, s // tq),
        in_specs=[blk(ATT_WIDTH), blk(IDX_HEADS * IDX_DIM), blk(LANES),
                  seq(ATT_WIDTH), seq(ATT_WIDTH), seq(IDX_DIM)],
        out_specs=blk(ATT_WIDTH),
        out_shape=jax.ShapeDtypeStruct((b, s, ATT_WIDTH), F32),
        scratch_shapes=[pltpu.VMEM((s // tk, tq, tk), F32), pltpu.VMEM((s // tk, tq, tk), F32)],
        compiler_params=_cparams(("parallel", "arbitrary")),
        name="prompt_dsa_attention",
    )(q, qi, kw, kb, vb, kib)


def _sample_select_kernel(pt_ref, *refs, n_steps, topk, t_new):
    del pt_ref
    pages = refs[:PAGES_PER_STEP]
    qi_ref, w_ref, kin_ref, bias_ref, score_ref = refs[PAGES_PER_STEP:]
    g = pl.program_id(1)
    n_pages = n_steps * PAGES_PER_STEP
    qi = qi_ref[0]
    w = w_ref[0]

    def combine(s):
        acc = jnp.zeros((8, LANES), F32)
        for h in range(IDX_HEADS):
            acc = acc + w[h * 8:(h + 1) * 8] * jnp.maximum(s[h * 8:(h + 1) * 8], 0.0)
        return acc

    for j in range(PAGES_PER_STEP):
        page_t = pages[j][...].astype(BF16)
        score_ref[g * PAGES_PER_STEP + j] = combine(_dot(qi, page_t))

    @pl.when(g == n_steps - 1)
    def _():
        s_new = combine(_dot_nt(qi, kin_ref[0]))
        lane = lax.broadcasted_iota(I32, (8, LANES), 1)
        row = lax.broadcasted_iota(I32, (8, LANES), 0)
        valid = jnp.logical_and(lane < t_new, lane <= row)
        score_ref[n_pages] = jnp.where(valid, s_new, -jnp.inf)

        def scan(body, init):
            return lax.fori_loop(0, n_pages + 1, lambda c, part: body(score_ref[c], part), init)
        thr, need = _kth_select(scan, 8, topk)
        upper = (lax.broadcasted_iota(I32, (LANES, LANES), 0)
                 < lax.broadcasted_iota(I32, (LANES, LANES), 1)).astype(BF16)

        def bias_page(c, before):
            sc = score_ref[c]
            eq = sc == thr
            eqf = jnp.where(eq, 1.0, 0.0)
            rank = before + _dot(eqf.astype(BF16), upper)
            keep = jnp.logical_or(sc > thr, jnp.logical_and(eq, rank < need))
            bias_ref[0, c] = jnp.where(keep, 0.0, NEG_BIG)
            return before + jnp.sum(eqf, axis=-1, keepdims=True)
        lax.fori_loop(0, n_pages + 1, bias_page, jnp.zeros((8, 1), F32))


def _sample_select(page_table, kidx_t, qi_rows, w_rows, ki_new, t_new):
    b, n_pages = page_table.shape
    page = kidx_t.shape[2]
    n_steps = n_pages // PAGES_PER_STEP
    topk = min(TOPK_MAX, (n_pages * page + t_new) // 4)

    def page_spec(j):
        return pl.BlockSpec((None, IDX_DIM, page),
                            lambda bi, g, pt: (pt[bi, g * PAGES_PER_STEP + j], 0, 0))
    per_seq = lambda shp: pl.BlockSpec((1,) + shp, lambda bi, g, pt: (bi,) + (0,) * len(shp))
    grid_spec = pltpu.PrefetchScalarGridSpec(
        num_scalar_prefetch=1,
        grid=(b, n_steps),
        in_specs=[page_spec(j) for j in range(PAGES_PER_STEP)]
                 + [per_seq((64, IDX_DIM)), per_seq((64, LANES)), per_seq((LANES, IDX_DIM))],
        out_specs=per_seq((n_pages + 1, 8, LANES)),
        scratch_shapes=[pltpu.VMEM((n_pages + 1, 8, LANES), F32)],
    )
    return pl.pallas_call(
        functools.partial(_sample_select_kernel, n_steps=n_steps, topk=topk, t_new=t_new),
        grid_spec=grid_spec,
        out_shape=jax.ShapeDtypeStruct((b, n_pages + 1, 8, LANES), F32),
        compiler_params=_cparams(("parallel", "arbitrary")),
        name="sample_dsa_select",
    )(page_table, *([kidx_t] * PAGES_PER_STEP), qi_rows, w_rows, ki_new)


def _sample_attn_kernel(pt_ref, *refs, n_steps):
    del pt_ref
    kpages = refs[:PAGES_PER_STEP]
    vpages = refs[PAGES_PER_STEP:2 * PAGES_PER_STEP]
    (q_ref, bias_ref, bias_new_ref, kn_ref, vn_ref, o_ref, m_ref, l_ref, acc_ref) = refs[2 * PAGES_PER_STEP:]
    g = pl.program_id(1)
    q = q_ref[0]

    @pl.when(g == 0)
    def _():
        m_ref[...] = jnp.full(m_ref.shape, NEG_BIG, F32)
        l_ref[...] = jnp.zeros(l_ref.shape, F32)
        acc_ref[...] = jnp.zeros(acc_ref.shape, F32)

    def update(s, bias8, pv):
        s = s + jnp.concatenate([bias8] * N_HEADS, axis=0)
        m = m_ref[...]
        m_new = jnp.maximum(m, jnp.max(s, axis=-1, keepdims=True))
        alpha = jnp.exp(m - m_new)
        pr = jnp.exp(s - m_new)
        l_ref[...] = alpha * l_ref[...] + jnp.sum(pr, axis=-1, keepdims=True)
        acc_ref[...] = alpha * acc_ref[...] + pv(pr.astype(BF16))
        m_ref[...] = m_new

    for j in range(PAGES_PER_STEP):
        kt = kpages[j][...].astype(BF16)
        vt = vpages[j][...].astype(BF16)
        update(_dot(q, kt), bias_ref[0, j], lambda pr, vt=vt: _dot_nt(pr, vt))

    @pl.when(g == n_steps - 1)
    def _():
        update(_dot_nt(q, kn_ref[0]), bias_new_ref[0, 0], lambda pr: _dot(pr, vn_ref[0]))
        o = acc_ref[...] / l_ref[...]
        lane = lax.broadcasted_iota(I32, (8, ATT_WIDTH), 1)
        out = jnp.zeros((8, ATT_WIDTH), F32)
        for h in range(N_HEADS):
            in_head = jnp.logical_and(lane >= h * HEAD_DIM, lane < (h + 1) * HEAD_DIM)
            out = out + jnp.where(in_head, o[h * 8:(h + 1) * 8], 0.0)
        o_ref[0] = out


def _sample_attention(page_table, k_t, v_t, q_bd, bias, k_new, v_new):
    b, n_pages = page_table.shape
    page = k_t.shape[2]
    n_steps = n_pages // PAGES_PER_STEP

    def page_spec(j):
        return pl.BlockSpec((None, ATT_WIDTH, page),
                            lambda bi, g, pt: (pt[bi, g * PAGES_PER_STEP + j], 0, 0))
    per_seq = lambda shp: pl.BlockSpec((1,) + shp, lambda bi, g, pt: (bi,) + (0,) * len(shp))
    grid_spec = pltpu.PrefetchScalarGridSpec(
        num_scalar_prefetch=1,
        grid=(b, n_steps),
        in_specs=[page_spec(j) for j in range(PAGES_PER_STEP)] * 2
                 + [per_seq((64, ATT_WIDTH)),
                    pl.BlockSpec((1, PAGES_PER_STEP, 8, LANES), lambda bi, g, pt: (bi, g, 0, 0)),
                    pl.BlockSpec((1, 1, 8, LANES), lambda bi, g, pt: (bi, n_pages, 0, 0)),
                    per_seq((LANES, ATT_WIDTH)), per_seq((LANES, ATT_WIDTH))],
        out_specs=per_seq((8, ATT_WIDTH)),
        scratch_shapes=[pltpu.VMEM((64, 1), F32), pltpu.VMEM((64, 1), F32),
                        pltpu.VMEM((64, ATT_WIDTH), F32)],
    )
    return pl.pallas_call(
        functools.partial(_sample_attn_kernel, n_steps=n_steps),
        grid_spec=grid_spec,
        out_shape=jax.ShapeDtypeStruct((b, 8, ATT_WIDTH), F32),
        compiler_params=_cparams(("parallel", "arbitrary")),
        name="sample_dsa_attention",
    )(page_table, *([k_t] * PAGES_PER_STEP), *([v_t] * PAGES_PER_STEP),
      q_bd, bias, bias, k_new, v_new)


def _conv_kernel(cur_ref, prev_ref, pre_ref, cw_ref, cb_ref, g_ref, b_ref, o_ref, win_ref, *, tt):
    i = pl.program_id(1)

    @pl.when(i == 0)
    def _():
        win_ref[0:HALO] = pre_ref[0]

    if tt >= HALO:
        @pl.when(i > 0)
        def _():
            win_ref[0:HALO] = prev_ref[0, tt - HALO:tt, :]

    win_ref[HALO:HALO + tt] = cur_ref[0]
    acc = jnp.zeros(o_ref.shape[1:], F32) + cb_ref[...]
    for j in range(CONV_WIDTH):
        acc = acc + win_ref[pl.ds(j + HALO - (CONV_WIDTH - 1), tt), :] * cw_ref[j:j + 1, :]
    mu = jnp.mean(acc, axis=-1, keepdims=True)
    xc = acc - mu
    y = xc * lax.rsqrt(jnp.mean(xc * xc, axis=-1, keepdims=True) + EPS)
    y = y * g_ref[...] + b_ref[...]
    o_ref[0] = y * jax.nn.sigmoid(y)


def _conv_module(glu, prefix, conv_w, conv_b, ln_g, ln_b, tt):
    b, t, c = glu.shape
    assert tt >= HALO or t == tt, (t, tt)
    full = lambda a: pl.BlockSpec(a.shape, lambda bi, i: (0,) * a.ndim)
    cw = jnp.zeros((HALO, c), F32).at[:CONV_WIDTH].set(conv_w)
    return pl.pallas_call(
        functools.partial(_conv_kernel, tt=tt),
        grid=(b, t // tt),
        in_specs=[pl.BlockSpec((1, tt, c), lambda bi, i: (bi, i, 0)),
                  pl.BlockSpec((1, tt, c), lambda bi, i: (bi, jnp.maximum(i - 1, 0), 0)),
                  pl.BlockSpec((1, HALO, c), lambda bi, i: (bi, 0, 0)),
                  full(cw), pl.BlockSpec((1, c), lambda bi, i: (0, 0)),
                  pl.BlockSpec((1, c), lambda bi, i: (0, 0)), pl.BlockSpec((1, c), lambda bi, i: (0, 0))],
        out_specs=pl.BlockSpec((1, tt, c), lambda bi, i: (bi, i, 0)),
        out_shape=jax.ShapeDtypeStruct((b, t, c), F32),
        scratch_shapes=[pltpu.VMEM((HALO + tt, c), F32)],
        compiler_params=_cparams(("parallel", "arbitrary")),
        name="conv_module",
    )(glu, glu, prefix, cw, conv_b.reshape(1, c), ln_g.reshape(1, c), ln_b.reshape(1, c))


def _outproj_kernel(att_ref, conv_ref, x_ref, g1_ref, sh_ref, sc_ref, ga_ref, gc_ref, woa_ref, woc_ref,
                    g2_ref, wrh_ref, wrl_ref, br_ref, x1_ref, h2_ref, gates_ref):
    a = (_rms(att_ref[...]) * ga_ref[...]).astype(BF16)
    c = (_rms(conv_ref[...]) * gc_ref[...]).astype(BF16)
    mixed = _dot(a, woa_ref[...]) + _dot(c, woc_ref[...])
    x1 = x_ref[...] + g1_ref[0] * mixed
    x1_ref[...] = x1
    h2 = (_rms(x1) * g2_ref[...]) * (1.0 + sc_ref[0]) + sh_ref[0]
    hi = h2.astype(BF16)
    h2_ref[...] = hi
    lo = (h2 - hi.astype(F32)).astype(BF16)
    logits = _dot(hi, wrh_ref[...]) + _dot(lo, wrh_ref[...]) + _dot(hi, wrl_ref[...]) + br_ref[...]
    lane = lax.broadcasted_iota(I32, logits.shape, 1).astype(F32)
    neg_inf = -jnp.inf
    far = float(4 * LANES)
    is_g = jnp.logical_and(lane >= N_EXPERTS, lane < N_EXPERTS + N_GROUPS)
    gl = jnp.where(is_g, logits, neg_inf)
    gmax = jnp.max(gl, axis=-1, keepdims=True)
    gsel = jnp.min(jnp.where(gl == gmax, lane, far), axis=-1, keepdims=True) - N_EXPERTS
    p_g = 1.0 / jnp.sum(jnp.exp(gl - gmax), axis=-1, keepdims=True)
    in_grp = jnp.logical_and(lane >= gsel * EXPERTS_PER_GROUP, lane < (gsel + 1.0) * EXPERTS_PER_GROUP)
    el = jnp.where(in_grp, logits, neg_inf)
    v1 = jnp.max(el, axis=-1, keepdims=True)
    i1 = jnp.min(jnp.where(el == v1, lane, far), axis=-1, keepdims=True)
    el2 = jnp.where(lane == i1, neg_inf, el)
    v2 = jnp.max(el2, axis=-1, keepdims=True)
    i2 = jnp.min(jnp.where(el2 == v2, lane, far), axis=-1, keepdims=True)
    e2 = jnp.exp(v2 - v1)
    den = 1.0 + e2
    gates_ref[...] = (jnp.where(lane == i1, (1.0 / den) * p_g, 0.0)
                      + jnp.where(lane == i2, (e2 / den) * p_g, 0.0))


def _outproj(att, conv, x, gate1, shift2, scale2, ga, gc, woa, woc, g2, wrh, wrl, br, tm, tiles_per_group):
    n, d = x.shape
    rows = gate1.shape[1]
    full = lambda a: pl.BlockSpec(a.shape, lambda i: (0,) * a.ndim)
    row = lambda w: pl.BlockSpec((tm, w), lambda i: (i, 0))
    ms = _mod_spec(rows, d, tiles_per_group)
    return pl.pallas_call(
        _outproj_kernel,
        grid=(n // tm,),
        in_specs=[row(att.shape[1]), row(conv.shape[1]), row(d), ms, ms, ms,
                  full(ga), full(gc), full(woa), full(woc), full(g2), full(wrh), full(wrl), full(br)],
        out_specs=[row(d), row(d), row(LANES)],
        out_shape=[jax.ShapeDtypeStruct((n, d), F32), jax.ShapeDtypeStruct((n, d), BF16),
                   jax.ShapeDtypeStruct((n, LANES), F32)],
        compiler_params=_cparams(("parallel",)),
        name="outproj_router",
    )(att, conv, x, gate1, shift2, scale2, ga, gc, woa, woc, g2, wrh, wrl, br)


def _moe_kernel(h_ref, gates_ref, w1_ref, w3_ref, w2_ref, x1_ref, g2_ref, gf_ref, y_ref, acc_ref):
    e = pl.program_id(1)

    @pl.when(e == 0)
    def _():
        acc_ref[...] = jnp.zeros(acc_ref.shape, F32)

    gates = gates_ref[...]
    lane = lax.broadcasted_iota(I32, gates.shape, 1)
    gcol = jnp.sum(jnp.where(lane == e, gates, 0.0), axis=-1, keepdims=True)

    hb = h_ref[...]
    a = _dot(hb, w1_ref[0])
    b = _dot(hb, w3_ref[0])
    hid = (a * jax.nn.sigmoid(a)) * b
    acc_ref[...] += gcol * _dot(hid.astype(BF16), w2_ref[0])

    @pl.when(e == pl.num_programs(1) - 1)
    def _():
        xf = x1_ref[...] + g2_ref[0] * acc_ref[...]
        y_ref[...] = _rms(xf) * gf_ref[...]


def _moe(h2, gates, w1, w3, w2, x1, gate2, g_final, tm, tiles_per_group):
    n, d = x1.shape
    n_e, _, f = w1.shape
    rows = gate2.shape[1]
    row = lambda w: pl.BlockSpec((tm, w), lambda i, e: (i, 0))
    return pl.pallas_call(
        _moe_kernel,
        grid=(n // tm, n_e),
        in_specs=[row(d), row(LANES),
                  pl.BlockSpec((1, d, f), lambda i, e: (e, 0, 0)),
                  pl.BlockSpec((1, d, f), lambda i, e: (e, 0, 0)),
                  pl.BlockSpec((1, f, d), lambda i, e: (e, 0, 0)),
                  row(d),
                  pl.BlockSpec((1, rows, d), lambda i, e: (i // tiles_per_group, 0, 0)),
                  pl.BlockSpec((1, d), lambda i, e: (0, 0))],
        out_specs=row(d),
        out_shape=jax.ShapeDtypeStruct((n, d), F32),
        scratch_shapes=[pltpu.VMEM((tm, d), F32)],
        compiler_params=_cparams(("parallel", "arbitrary")),
        name="moe_final",
    )(h2, gates, w1, w3, w2, x1, gate2, g_final)


def _tile(n, pref):
    t = min(n, pref)
    assert n % t == 0, (n, pref)
    return t


def _prep_weights(w_in, w_out, w_rg, b_rg, w_re, b_re, w1, w3, w2):
    d = w_in.shape[0]
    off_qi = 3 * ATT_WIDTH
    off_ki = off_qi + IDX_HEADS * IDX_DIM
    off_conv = off_ki + IDX_DIM + IDX_HEADS
    cc = (w_in.shape[1] - off_conv) // 2
    wb = w_in.astype(BF16)
    wkw = jnp.zeros((d, LANES), BF16).at[:, :IDX_DIM + IDX_HEADS].set(wb[:, off_ki:off_conv])
    in_w = (wb[:, :off_qi], wb[:, off_qi:off_ki], wkw, wb[:, off_conv:off_conv + cc], wb[:, off_conv + cc:])
    wob = w_out.astype(BF16)
    out_w = (wob[:ATT_WIDTH], wob[ATT_WIDTH:])
    wr = jnp.zeros((d, LANES), F32)
    wr = wr.at[:, :N_EXPERTS].set(jnp.transpose(w_re, (1, 0, 2)).reshape(d, N_EXPERTS))
    wr = wr.at[:, N_EXPERTS:N_EXPERTS + N_GROUPS].set(w_rg)
    br = jnp.zeros((1, LANES), F32).at[0, :N_EXPERTS].set(b_re.reshape(-1))
    br = br.at[0, N_EXPERTS:N_EXPERTS + N_GROUPS].set(b_rg)
    wrh = wr.astype(BF16)
    wrl = (wr - wrh.astype(F32)).astype(BF16)
    return in_w, out_w, (wrh, wrl, br), (w1.astype(BF16), w3.astype(BF16), w2.astype(BF16))


def _token_layers(x2, mod_rows, tm, tm_moe, attn_fn, conv_fn, norms, weights):
    g1, ga, gc, g2, gf = norms
    in_w, out_w, router_w, expert_w = weights
    (shift1, tpg), (scale1, _), (gate1, _), (shift2, _), (scale2, _) = [mod_rows(j, tm) for j in range(5)]
    gate2, tpg_moe = mod_rows(5, tm_moe)
    q, k, v, kb, vb, qi, kw, kib, glu = _inproj(x2, shift1, scale1, g1, in_w, tm, tpg)
    att = attn_fn(q, k, v, kb, vb, qi, kw, kib)
    conv = conv_fn(glu)
    x1, h2, gates = _outproj(att, conv, x2, gate1, shift2, scale2, ga, gc, out_w[0], out_w[1], g2,
                             *router_w, tm, tpg)
    y = _moe(h2, gates, *expert_w, x1, gate2, gf, tm_moe, tpg_moe)
    return y, k, v, kw, glu


def kernel(x_prompt, x_sample, c_prompt, c_sample, cache_k, cache_v, cache_kidx, state_conv, page_table,
           w_ada, b_ada, g_norm1, w_in, conv_w, conv_b, conv_ln_g, conv_ln_b, g_attn_out, g_conv_out,
           w_out, g_norm2, w_rg, b_rg, w_re, b_re, w1, w3, w2, g_final):
    bp, s, d = x_prompt.shape
    bs, t_new, _ = x_sample.shape
    cc = conv_w.shape[1]
    n_pool, page = cache_k.shape[:2]
    hist = CONV_WIDTH - 1

    weights = _prep_weights(w_in, w_out, w_rg, b_rg, w_re, b_re, w1, w3, w2)
    norms = (g_norm1.reshape(1, d), g_attn_out.reshape(1, ATT_WIDTH), g_conv_out.reshape(1, cc),
             g_norm2.reshape(1, d), g_final.reshape(1, d))

    n_c = bp + bs
    pad = (-n_c) % 8
    c_all = jnp.concatenate([c_prompt, c_sample, jnp.zeros((pad, d), F32)], axis=0)
    mod = _modulation(c_all, w_ada, b_ada).reshape(n_c + pad, 6, d)

    tm_p = _tile(s, 256)

    def mod_p(j, tile):
        return mod[:bp, j].reshape(bp, 1, d), s // tile

    def prompt_attn(q, k, v, kb, vb, qi, kw, kib):
        r3 = lambda a: a.reshape(bp, s, a.shape[-1])
        tq = _tile(s, 256)
        tk = _tile(s, 512)
        return _prompt_attention(r3(q), r3(qi), r3(kw), r3(kb), r3(vb), r3(kib), tq, tk).reshape(bp * s, ATT_WIDTH)

    def prompt_conv(glu):
        prefix = jnp.zeros((bp, HALO, cc), F32)
        tt = _tile(s, 512)
        return _conv_module(glu.reshape(bp, s, cc), prefix, conv_w, conv_b, conv_ln_g, conv_ln_b, tt).reshape(bp * s, cc)

    y_p, k_p, v_p, kw_p, glu_p = _token_layers(
        x_prompt.reshape(bp * s, d), mod_p, tm_p, _tile(s, 1024), prompt_attn, prompt_conv, norms, weights)

    kidx_t = jnp.transpose(cache_kidx, (0, 2, 1))
    k_t = jnp.transpose(cache_k, (0, 2, 3, 1)).reshape(n_pool, ATT_WIDTH, page)
    v_t = jnp.transpose(cache_v, (0, 2, 3, 1)).reshape(n_pool, ATT_WIDTH, page)
    n_s = bs * t_new
    tm_s = _tile(n_s, 256)
    mod_tok = jnp.repeat(mod[bp:bp + bs], t_new, axis=0)

    def mod_s(j, tile):
        return mod_tok[:, j].reshape(n_s // tile, tile, d), 1

    def sample_attn(q, k, v, kb, vb, qi, kw, kib):
        tp = 8 - t_new
        qi_rows = jnp.pad(qi.reshape(bs, t_new, IDX_HEADS, IDX_DIM).transpose(0, 2, 1, 3),
                          ((0, 0), (0, 0), (0, tp), (0, 0))).reshape(bs, 64, IDX_DIM)
        wi = kw[:, IDX_DIM:IDX_DIM + IDX_HEADS].reshape(bs, t_new, IDX_HEADS).transpose(0, 2, 1)
        w_rows = jnp.broadcast_to(jnp.pad(wi, ((0, 0), (0, 0), (0, tp))).reshape(bs, 64, 1), (bs, 64, LANES))
        rows_pad = lambda a: jnp.pad(a.reshape(bs, t_new, a.shape[-1]), ((0, 0), (0, LANES - t_new), (0, 0)))
        bias = _sample_select(page_table, kidx_t, qi_rows, w_rows, rows_pad(kib), t_new)
        qh = jnp.pad(q.reshape(bs, t_new, N_HEADS, HEAD_DIM).transpose(0, 2, 1, 3),
                     ((0, 0), (0, 0), (0, tp), (0, 0)))
        eye = jnp.eye(N_HEADS, dtype=q.dtype)
        q_bd = (qh[:, :, :, None, :] * eye[None, :, None, :, None]).reshape(bs, 64, ATT_WIDTH)
        o = _sample_attention(page_table, k_t, v_t, q_bd, bias, rows_pad(kb), rows_pad(vb))
        return o[:, :t_new].reshape(n_s, ATT_WIDTH)

    def sample_conv(glu):
        prefix = jnp.pad(state_conv, ((0, 0), (HALO - hist, 0), (0, 0)))
        glu8 = jnp.pad(glu.reshape(bs, t_new, cc), ((0, 0), (0, 8 - t_new), (0, 0)))
        return _conv_module(glu8, prefix, conv_w, conv_b, conv_ln_g, conv_ln_b, 8)[:, :t_new].reshape(n_s, cc)

    y_s, k_s, v_s, kw_s, glu_s = _token_layers(
        x_sample.reshape(n_s, d), mod_s, tm_s, _tile(n_s, 1024), sample_attn, sample_conv, norms, weights)

    glu_p3 = glu_p.reshape(bp, s, cc)
    conv_prompt = glu_p3[:, s - hist:]
    conv_sample = jnp.concatenate([state_conv, glu_s.reshape(bs, t_new, cc)], axis=1)[:, -hist:]
    return (y_p.reshape(bp, s, d), y_s.reshape(bs, t_new, d),
            k_p.reshape(bp, s, N_HEADS, HEAD_DIM), v_p.reshape(bp, s, N_HEADS, HEAD_DIM),
            kw_p[:, :IDX_DIM].reshape(bp, s, IDX_DIM), conv_prompt,
            k_s.reshape(bs, t_new, N_HEADS, HEAD_DIM), v_s.reshape(bs, t_new, N_HEADS, HEAD_DIM),
            kw_s[:, :IDX_DIM].reshape(bs, t_new, IDX_DIM), conv_sample)
```

```python
import functools

import jax
import jax.numpy as jnp
from jax import lax
from jax.experimental import pallas as pl
from jax.experimental.pallas import tpu as pltpu

F32 = jnp.float32
BF16 = jnp.bfloat16
I32 = jnp.int32

EPS = 1e-6
HEAD_DIM = 64
N_HEADS = 8
ATT_WIDTH = N_HEADS * HEAD_DIM
IDX_HEADS = 8
IDX_DIM = 64
CONV_WIDTH = 31
TOPK_MAX = 256
N_GROUPS = 4
EXPERTS_PER_GROUP = 8
N_EXPERTS = N_GROUPS * EXPERTS_PER_GROUP
LANES = 128
HALO = 32
PAGES_PER_STEP = 16
SELECT_GROUP = 16
SELECT_ROWS = 128
PAIRS_PER_LOOP = 2
NEG_BIG = -1e30
MAX_BISECT = 40
KEEP_ALL = 1e9
VMEM_LIMIT = 56 * 1024 * 1024


def _cparams(sem):
    return pltpu.CompilerParams(dimension_semantics=sem, vmem_limit_bytes=VMEM_LIMIT)


def _rms(x):
    return x * lax.rsqrt(jnp.mean(x * x, axis=-1, keepdims=True) + EPS)


def _dot(a, b):
    return jnp.dot(a, b, preferred_element_type=F32)


def _dot_nt(a, b):
    return lax.dot_general(a, b, (((1,), (1,)), ((), ())), preferred_element_type=F32)


def _mod_kernel(c_ref, w_ref, b_ref, o_ref):
    c = c_ref[...]
    a = (c * jax.nn.sigmoid(c)).astype(BF16)
    o_ref[...] = _dot(a, w_ref[...].astype(BF16)) + b_ref[...]


def _modulation(c, w_ada, b_ada, tn=512):
    r, d = c.shape
    n = w_ada.shape[1]
    return pl.pallas_call(
        _mod_kernel,
        grid=(n // tn,),
        in_specs=[pl.BlockSpec((r, d), lambda j: (0, 0)),
                  pl.BlockSpec((d, tn), lambda j: (0, j)),
                  pl.BlockSpec((1, tn), lambda j: (0, j))],
        out_specs=pl.BlockSpec((r, tn), lambda j: (0, j)),
        out_shape=jax.ShapeDtypeStruct((r, n), F32),
        compiler_params=_cparams(("arbitrary",)),
        name="adaln_mod",
    )(c, w_ada, b_ada.reshape(1, n))


def _inproj_kernel(x_ref, sh_ref, sc_ref, g_ref, wqkv_ref, wqi_ref, wkw_ref, wca_ref, wcb_ref,
                   q_ref, k_ref, v_ref, kb_ref, vb_ref, qi_ref, kw_ref, kib_ref, glu_ref):
    h = _rms(x_ref[...]) * g_ref[...]
    h = h * (1.0 + sc_ref[0]) + sh_ref[0]
    hb = h.astype(BF16)
    qkv = _dot(hb, wqkv_ref[...])
    q_ref[...] = (qkv[:, :ATT_WIDTH] * (HEAD_DIM ** -0.5)).astype(BF16)
    k = qkv[:, ATT_WIDTH:2 * ATT_WIDTH]
    v = qkv[:, 2 * ATT_WIDTH:]
    k_ref[...] = k
    v_ref[...] = v
    kb_ref[...] = k.astype(BF16)
    vb_ref[...] = v.astype(BF16)
    qi_ref[...] = _dot(hb, wqi_ref[...]).astype(BF16)
    kw = _dot(hb, wkw_ref[...])
    kw_ref[...] = kw
    kib_ref[...] = kw[:, :IDX_DIM].astype(BF16)
    a = _dot(hb, wca_ref[...])
    b = _dot(hb, wcb_ref[...])
    glu_ref[...] = a * jax.nn.sigmoid(b)


def _mod_spec(rows, d, tiles_per_group):
    return pl.BlockSpec((1, rows, d), lambda i: (i // tiles_per_group, 0, 0))


def _inproj(x, shift, scale, g1, wts, tm, tiles_per_group):
    n, d = x.shape
    wqkv, wqi, wkw, wca, wcb = wts
    cc = wca.shape[1]
    rows = shift.shape[1]
    full = lambda a: pl.BlockSpec(a.shape, lambda i: (0,) * a.ndim)
    row = lambda w: pl.BlockSpec((tm, w), lambda i: (i, 0))
    outs = [
        (ATT_WIDTH, BF16),
        (ATT_WIDTH, F32),
        (ATT_WIDTH, F32),
        (ATT_WIDTH, BF16),
        (ATT_WIDTH, BF16),
        (IDX_HEADS * IDX_DIM, BF16),
        (LANES, F32),
        (IDX_DIM, BF16),
        (cc, F32),
    ]
    return pl.pallas_call(
        _inproj_kernel,
        grid=(n // tm,),
        in_specs=[row(d), _mod_spec(rows, d, tiles_per_group), _mod_spec(rows, d, tiles_per_group),
                  full(g1), full(wqkv), full(wqi), full(wkw), full(wca), full(wcb)],
        out_specs=[row(w) for w, _ in outs],
        out_shape=[jax.ShapeDtypeStruct((n, w), dt) for w, dt in outs],
        compiler_params=_cparams(("parallel",)),
        name="inproj",
    )(x, shift, scale, g1, wqkv, wqi, wkw, wca, wcb)


def _fold(scan, rows, init, fn, red):
    def body(sc, part, sl=slice(None)):
        for g in range(sc.shape[-1] // LANES):
            part = fn(part, sc[..., g * LANES:(g + 1) * LANES], sl)
        return part
    return red(scan(body, jnp.full(rows + (LANES,), init, F32)), axis=-1, keepdims=True)


def _kth_select(scan, rows, k):
    inf = jnp.inf
    count_ge = lambda t: _fold(scan, rows, 0.0, lambda p, x, sl: p + jnp.where(x >= t[sl], 1.0, 0.0), jnp.sum)
    count_gt = lambda t: _fold(scan, rows, 0.0, lambda p, x, sl: p + jnp.where(x > t[sl], 1.0, 0.0), jnp.sum)
    min_ge = lambda t: _fold(scan, rows, inf,
                             lambda p, x, sl: jnp.minimum(p, jnp.where(x >= t[sl], x, inf)), jnp.min)
    min_gt = lambda t: _fold(scan, rows, inf,
                             lambda p, x, sl: jnp.minimum(p, jnp.where(x > t[sl], x, inf)), jnp.min)
    row_max = _fold(scan, rows, -inf, lambda p, x, sl: jnp.maximum(p, x), jnp.max)
    row_min = min_gt(jnp.full(rows + (1,), -inf, F32))
    n_valid = count_ge(row_min)
    any_row = lambda flag: jnp.max(flag) > 0.5

    def bisect(lo, hi, cnt, pending):
        def cond(st):
            it, _, _, _, act = st
            return jnp.logical_and(it < MAX_BISECT, any_row(act))

        def body(st):
            it, lo, hi, cnt, act = st
            mid = 0.5 * (lo + hi)
            c = count_ge(mid)
            go_up = jnp.where(c >= k, act, 0.0) > 0.5
            go_dn = jnp.where(c >= k, 0.0, act) > 0.5
            lo2 = jnp.where(go_up, mid, lo)
            cnt2 = jnp.where(go_up, c, cnt)
            hi2 = jnp.where(go_dn, mid, hi)
            moving = jnp.logical_and(jnp.logical_and(mid > lo, mid < hi), cnt2 > k)
            return it + 1, lo2, hi2, cnt2, jnp.where(moving, act, 0.0)

        act0 = jnp.where(cnt > k, pending, 0.0)
        _, lo, hi, cnt, _ = lax.while_loop(cond, body, (jnp.int32(0), lo, hi, cnt, act0))
        return lo, hi, cnt

    def outer_cond(st):
        return any_row(st[5])

    def outer_body(st):
        lo, hi, cnt, thr, need, pending = st
        lo, hi, cnt = bisect(lo, hi, cnt, pending)
        plain = jnp.where(cnt <= k, pending, 0.0)
        hard = jnp.where(cnt <= k, 0.0, pending)
        thr = jnp.where(plain > 0.5, lo, thr)

        def resolve(_):
            t = min_ge(lo)
            n_gt = count_gt(t)
            tied = jnp.where(n_gt < k, hard, 0.0) > 0.5
            above_f = jnp.where(n_gt < k, 0.0, hard)
            above = above_f > 0.5
            return (jnp.where(above, min_gt(t), lo), jnp.where(above, n_gt, cnt),
                    jnp.where(tied, t, thr), jnp.where(tied, k - n_gt, need), above_f)

        def settled(_):
            return lo, cnt, thr, need, jnp.zeros(rows + (1,), F32)

        lo, cnt, thr, need, pending = lax.cond(any_row(hard), resolve, settled, 0)
        return lo, hi, cnt, thr, need, pending

    lo0 = row_min
    hi0 = row_max + (row_max - row_min) + 1.0
    st0 = (lo0, hi0, n_valid, lo0, jnp.full(rows + (1,), KEEP_ALL, F32), jnp.ones(rows + (1,), F32))
    _, _, _, thr, need, _ = lax.while_loop(outer_cond, outer_body, st0)
    return thr, need


def _prompt_attn_kernel(q_ref, qi_ref, kw_ref, kb_ref, vb_ref, kib_ref, o_ref,
                        score_ref, bias_ref, *, tq, tk, topk):
    i = pl.program_id(1)
    nk = ((i + 1) * tq + tk - 1) // tk
    lane = lax.broadcasted_iota(I32, (tq, tk), 1)
    qpos = i * tq + lax.broadcasted_iota(I32, (tq, tk), 0)
    wi = kw_ref[0][:, IDX_DIM:IDX_DIM + IDX_HEADS]
    qi = qi_ref[0]

    def score_chunk(c, carry):
        kic = kib_ref[0, pl.ds(pl.multiple_of(c * tk, tk), tk), :]
        acc = jnp.zeros((tq, tk), F32)
        for h in range(IDX_HEADS):
            s = _dot_nt(qi[:, h * IDX_DIM:(h + 1) * IDX_DIM], kic)
            acc = acc + wi[:, h:h + 1] * jnp.maximum(s, 0.0)
        valid = (c * tk + lane) <= qpos
        score_ref[c] = jnp.where(valid, acc, -jnp.inf)
        return carry
    lax.fori_loop(0, nk, score_chunk, 0)

    def scan(body, init):
        outs = []
        for r in range(0, tq, SELECT_ROWS):
            sl = slice(r, min(r + SELECT_ROWS, tq))
            outs.append(lax.fori_loop(0, nk, lambda c, part, sl=sl: body(score_ref[c, sl, :], part, sl),
                                      init[sl]))
        return jnp.concatenate(outs, axis=0)
    thr, need = _kth_select(scan, (tq,), topk)

    def bias_chunk(c, carry):
        bias_ref[c] = jnp.where(score_ref[c] >= thr, 0.0, NEG_BIG)
        return carry
    lax.fori_loop(0, nk, bias_chunk, 0)

    tie_f = jnp.where(need < KEEP_ALL, 1.0, 0.0)

    @pl.when(jnp.max(tie_f) > 0.5)
    def _():
        upper = (lax.broadcasted_iota(I32, (tk, tk), 0)
                 < lax.broadcasted_iota(I32, (tk, tk), 1)).astype(BF16)

        def tie_chunk(c, before):
            sc = score_ref[c]
            eq = sc == thr
            eqf = jnp.where(eq, 1.0, 0.0)
            rank = before + _dot(eqf.astype(BF16), upper)
            tie_bias = jnp.where(jnp.logical_or(sc > thr, jnp.logical_and(eq, rank < need)), 0.0, NEG_BIG)
            bias_ref[c] = jnp.where(tie_f > 0.5, tie_bias, bias_ref[c])
            return before + jnp.sum(eqf, axis=-1, keepdims=True)
        lax.fori_loop(0, nk, tie_chunk, jnp.zeros((tq, 1), F32))

    half = lax.broadcasted_iota(I32, (tq, LANES), 1) < HEAD_DIM
    for p0 in range(0, N_HEADS // 2, PAIRS_PER_LOOP):
        slabs = [slice(p * LANES, (p + 1) * LANES) for p in range(p0, p0 + PAIRS_PER_LOOP)]
        qs = []
        for sl in slabs:
            qp = q_ref[0, :, sl]
            qs.append((jnp.where(half, qp, jnp.zeros_like(qp)), jnp.where(half, jnp.zeros_like(qp), qp)))

        def attn_chunk(c, carry):
            off = pl.multiple_of(c * tk, tk)
            bias = bias_ref[c]
            new = []
            for n, sl in enumerate(slabs):
                kc = kb_ref[0, pl.ds(off, tk), sl]
                vc = vb_ref[0, pl.ds(off, tk), sl]
                for e in range(2):
                    m, l, acc = carry[2 * n + e]
                    s = _dot_nt(qs[n][e], kc) + bias
                    m_new = jnp.maximum(m, jnp.max(s, axis=-1, keepdims=True))
                    alpha = jnp.exp(m - m_new)
                    pr = jnp.exp(s - m_new)
                    l = alpha * l + jnp.sum(pr, axis=-1, keepdims=True)
                    acc = alpha * acc + _dot(pr.astype(BF16), vc)
                    new.append((m_new, l, acc))
            return tuple(new)

        init = tuple((jnp.full((tq, 1), NEG_BIG, F32), jnp.zeros((tq, 1), F32),
                      jnp.zeros((tq, LANES), F32)) for _ in range(2 * PAIRS_PER_LOOP))
        res = lax.fori_loop(0, nk, attn_chunk, init)
        for n, sl in enumerate(slabs):
            (_, l0, a0), (_, l1, a1) = res[2 * n], res[2 * n + 1]
            o_ref[0, :, sl] = jnp.where(half, a0 / l0, a1 / l1)


def _prompt_attention(q, qi, kw, kb, vb, kib, tq, tk):
    b, s, _ = q.shape
    topk = min(TOPK_MAX, s // 4)
    blk = lambda w: pl.BlockSpec((1, tq, w), lambda bi, i: (bi, i, 0))
    seq = lambda w: pl.BlockSpec((1, s, w), lambda bi, i: (bi, 0, 0))
    return pl.pallas_call(
        functools.partial(_prompt_attn_kernel, tq=tq, tk=tk, topk=topk),
        grid=(b, s // tq),
        in_specs=[blk(ATT_WIDTH), blk(IDX_HEADS * IDX_DIM), blk(LANES),
                  seq(ATT_WIDTH), seq(ATT_WIDTH), seq(IDX_DIM)],
        out_specs=blk(ATT_WIDTH),
        out_shape=jax.ShapeDtypeStruct((b, s, ATT_WIDTH), F32),
        scratch_shapes=[pltpu.VMEM((s // tk, tq, tk), F32), pltpu.VMEM((s // tk, tq, tk), F32)],
        compiler_params=_cparams(("parallel", "arbitrary")),
        name="prompt_dsa_attention",
    )(q, qi, kw, kb, vb, kib)


def _sample_scores_kernel(pt_ref, *refs, n_pages, t_new):
    del pt_ref
    pages = refs[:n_pages]
    qi_ref, w_ref, kin_ref, score_ref = refs[n_pages:]
    qi = qi_ref[0]
    w = w_ref[0]

    def combine(s):
        acc = jnp.zeros((8, LANES), F32)
        for h in range(IDX_HEADS):
            acc = acc + w[h * 8:(h + 1) * 8] * jnp.maximum(s[h * 8:(h + 1) * 8], 0.0)
        return acc

    for j in range(n_pages):
        page_t = pages[j][...].astype(BF16)
        score_ref[j] = combine(_dot(qi, page_t))

    s_new = combine(_dot_nt(qi, kin_ref[0]))
    lane = lax.broadcasted_iota(I32, (8, LANES), 1)
    row = lax.broadcasted_iota(I32, (8, LANES), 0)
    valid = jnp.logical_and(lane < t_new, lane <= row)
    score_ref[n_pages] = jnp.where(valid, s_new, -jnp.inf)


def _sample_scores(page_table, kidx_t, qi_rows, w_rows, ki_new, t_new):
    b, n_pages = page_table.shape
    page = kidx_t.shape[2]

    def page_spec(j):
        return pl.BlockSpec((None, IDX_DIM, page), lambda bi, pt: (pt[bi, j], 0, 0))
    per_seq = lambda shp: pl.BlockSpec((1,) + shp, lambda bi, pt: (bi,) + (0,) * len(shp))
    grid_spec = pltpu.PrefetchScalarGridSpec(
        num_scalar_prefetch=1,
        grid=(b,),
        in_specs=[page_spec(j) for j in range(n_pages)]
                 + [per_seq((64, IDX_DIM)), per_seq((64, LANES)), per_seq((LANES, IDX_DIM))],
        out_specs=pl.BlockSpec((n_pages + 1, None, 8, LANES), lambda bi, pt: (0, bi, 0, 0)),
    )
    return pl.pallas_call(
        functools.partial(_sample_scores_kernel, n_pages=n_pages, t_new=t_new),
        grid_spec=grid_spec,
        out_shape=jax.ShapeDtypeStruct((n_pages + 1, b, 8, LANES), F32),
        compiler_params=_cparams(("parallel",)),
        name="sample_dsa_scores",
    )(page_table, *([kidx_t] * n_pages), qi_rows, w_rows, ki_new)


def _sample_select_kernel(score_ref, bias_ref, *, topk):
    n_chunks, g = score_ref.shape[:2]

    def scan(body, init):
        return lax.fori_loop(0, n_chunks, lambda c, part: body(score_ref[c], part), init)
    thr, need = _kth_select(scan, (g, 8), topk)

    def bias_page(c, carry):
        bias_ref[c] = jnp.where(score_ref[c] >= thr, 0.0, NEG_BIG)
        return carry
    lax.fori_loop(0, n_chunks, bias_page, 0)

    tie_f = jnp.where(need < KEEP_ALL, 1.0, 0.0)

    @pl.when(jnp.max(tie_f) > 0.5)
    def _():
        upper = (lax.broadcasted_iota(I32, (LANES, LANES), 0)
                 < lax.broadcasted_iota(I32, (LANES, LANES), 1)).astype(BF16)

        def tie_page(c, before):
            sc = score_ref[c]
            eq = sc == thr
            eqf = jnp.where(eq, 1.0, 0.0)
            low = _dot(eqf.reshape(g * 8, LANES).astype(BF16), upper).reshape(g, 8, LANES)
            rank = before + low
            tie_bias = jnp.where(jnp.logical_or(sc > thr, jnp.logical_and(eq, rank < need)), 0.0, NEG_BIG)
            bias_ref[c] = jnp.where(tie_f > 0.5, tie_bias, bias_ref[c])
            return before + jnp.sum(eqf, axis=-1, keepdims=True)
        lax.fori_loop(0, n_chunks, tie_page, jnp.zeros((g, 8, 1), F32))


def _sample_select(scores, topk, group):
    n_chunks, b = scores.shape[:2]
    spec = pl.BlockSpec((n_chunks, group, 8, LANES), lambda i: (0, i, 0, 0))
    return pl.pallas_call(
        functools.partial(_sample_select_kernel, topk=topk),
        grid=(b // group,),
        in_specs=[spec],
        out_specs=spec,
        out_shape=jax.ShapeDtypeStruct(scores.shape, F32),
        compiler_params=_cparams(("parallel",)),
        name="sample_dsa_select",
    )(scores)


def _sample_attn_kernel(pt_ref, *refs, n_steps):
    del pt_ref
    kpages = refs[:PAGES_PER_STEP]
    vpages = refs[PAGES_PER_STEP:2 * PAGES_PER_STEP]
    (q_ref, bias_ref, bias_new_ref, kn_ref, vn_ref, o_ref,
     kt_ref, vt_ref, m_ref, l_ref, acc_ref) = refs[2 * PAGES_PER_STEP:]
    g = pl.program_id(1)
    q = q_ref[0]

    @pl.when(g == 0)
    def _():
        m_ref[...] = jnp.full(m_ref.shape, NEG_BIG, F32)
        l_ref[...] = jnp.zeros(l_ref.shape, F32)
        acc_ref[...] = jnp.zeros(acc_ref.shape, F32)

    def update(s, pv):
        m = m_ref[...]
        m_new = jnp.maximum(m, jnp.max(s, axis=-1, keepdims=True))
        alpha = jnp.exp(m - m_new)
        pr = jnp.exp(s - m_new)
        l_ref[...] = alpha * l_ref[...] + jnp.sum(pr, axis=-1, keepdims=True)
        acc_ref[...] = alpha * acc_ref[...] + pv(pr.astype(BF16))
        m_ref[...] = m_new

    tile8 = lambda b8: jnp.concatenate([b8] * N_HEADS, axis=0)

    for j in range(PAGES_PER_STEP):
        kt_ref[:, j * LANES:(j + 1) * LANES] = kpages[j][...].astype(BF16)
        vt_ref[:, j * LANES:(j + 1) * LANES] = vpages[j][...].astype(BF16)
    s = _dot(q, kt_ref[...])
    s = jnp.concatenate([s[:, j * LANES:(j + 1) * LANES] + tile8(bias_ref[j])
                         for j in range(PAGES_PER_STEP)], axis=1)
    update(s, lambda pr: _dot_nt(pr, vt_ref[...]))

    @pl.when(g == n_steps - 1)
    def _():
        update(_dot_nt(q, kn_ref[0]) + tile8(bias_new_ref[0]), lambda pr: _dot(pr, vn_ref[0]))
        o = acc_ref[...] / l_ref[...]
        lane = lax.broadcasted_iota(I32, (8, ATT_WIDTH), 1)
        out = jnp.zeros((8, ATT_WIDTH), F32)
        for h in range(N_HEADS):
            in_head = jnp.logical_and(lane >= h * HEAD_DIM, lane < (h + 1) * HEAD_DIM)
            out = out + jnp.where(in_head, o[h * 8:(h + 1) * 8], 0.0)
        o_ref[0] = out


def _sample_attention(page_table, k_t, v_t, q_bd, bias, k_new, v_new):
    b, n_pages = page_table.shape
    page = k_t.shape[2]
    n_steps = n_pages // PAGES_PER_STEP

    def page_spec(j):
        return pl.BlockSpec((None, ATT_WIDTH, page),
                            lambda bi, g, pt: (pt[bi, g * PAGES_PER_STEP + j], 0, 0))
    per_seq = lambda shp: pl.BlockSpec((1,) + shp, lambda bi, g, pt: (bi,) + (0,) * len(shp))
    grid_spec = pltpu.PrefetchScalarGridSpec(
        num_scalar_prefetch=1,
        grid=(b, n_steps),
        in_specs=[page_spec(j) for j in range(PAGES_PER_STEP)] * 2
                 + [per_seq((64, ATT_WIDTH)),
                    pl.BlockSpec((PAGES_PER_STEP, None, 8, LANES), lambda bi, g, pt: (g, bi, 0, 0)),
                    pl.BlockSpec((1, None, 8, LANES), lambda bi, g, pt: (n_pages, bi, 0, 0)),
                    per_seq((LANES, ATT_WIDTH)), per_seq((LANES, ATT_WIDTH))],
        out_specs=per_seq((8, ATT_WIDTH)),
        scratch_shapes=[pltpu.VMEM((ATT_WIDTH, PAGES_PER_STEP * page), BF16),
                        pltpu.VMEM((ATT_WIDTH, PAGES_PER_STEP * page), BF16),
                        pltpu.VMEM((64, 1), F32), pltpu.VMEM((64, 1), F32),
                        pltpu.VMEM((64, ATT_WIDTH), F32)],
    )
    return pl.pallas_call(
        functools.partial(_sample_attn_kernel, n_steps=n_steps),
        grid_spec=grid_spec,
        out_shape=jax.ShapeDtypeStruct((b, 8, ATT_WIDTH), F32),
        compiler_params=_cparams(("parallel", "arbitrary")),
        name="sample_dsa_attention",
    )(page_table, *([k_t] * PAGES_PER_STEP), *([v_t] * PAGES_PER_STEP),
      q_bd, bias, bias, k_new, v_new)


def _conv_kernel(cur_ref, prev_ref, pre_ref, cw_ref, cb_ref, g_ref, b_ref, o_ref, win_ref, *, tt):
    i = pl.program_id(1)

    @pl.when(i == 0)
    def _():
        win_ref[0:HALO] = pre_ref[0]

    if tt >= HALO:
        @pl.when(i > 0)
        def _():
            win_ref[0:HALO] = prev_ref[0, tt - HALO:tt, :]

    win_ref[HALO:HALO + tt] = cur_ref[0]
    acc = jnp.zeros(o_ref.shape[1:], F32) + cb_ref[...]
    for j in range(CONV_WIDTH):
        acc = acc + win_ref[pl.ds(j + HALO - (CONV_WIDTH - 1), tt), :] * cw_ref[j:j + 1, :]
    mu = jnp.mean(acc, axis=-1, keepdims=True)
    xc = acc - mu
    y = xc * lax.rsqrt(jnp.mean(xc * xc, axis=-1, keepdims=True) + EPS)
    y = y * g_ref[...] + b_ref[...]
    o_ref[0] = y * jax.nn.sigmoid(y)


def _conv_module(glu, prefix, conv_w, conv_b, ln_g, ln_b, tt):
    b, t, c = glu.shape
    assert tt >= HALO or t == tt, (t, tt)
    full = lambda a: pl.BlockSpec(a.shape, lambda bi, i: (0,) * a.ndim)
    cw = jnp.zeros((HALO, c), F32).at[:CONV_WIDTH].set(conv_w)
    return pl.pallas_call(
        functools.partial(_conv_kernel, tt=tt),
        grid=(b, t // tt),
        in_specs=[pl.BlockSpec((1, tt, c), lambda bi, i: (bi, i, 0)),
                  pl.BlockSpec((1, tt, c), lambda bi, i: (bi, jnp.maximum(i - 1, 0), 0)),
                  pl.BlockSpec((1, HALO, c), lambda bi, i: (bi, 0, 0)),
                  full(cw), pl.BlockSpec((1, c), lambda bi, i: (0, 0)),
                  pl.BlockSpec((1, c), lambda bi, i: (0, 0)), pl.BlockSpec((1, c), lambda bi, i: (0, 0))],
        out_specs=pl.BlockSpec((1, tt, c), lambda bi, i: (bi, i, 0)),
        out_shape=jax.ShapeDtypeStruct((b, t, c), F32),
        scratch_shapes=[pltpu.VMEM((HALO + tt, c), F32)],
        compiler_params=_cparams(("parallel", "arbitrary")),
        name="conv_module",
    )(glu, glu, prefix, cw, conv_b.reshape(1, c), ln_g.reshape(1, c), ln_b.reshape(1, c))


def _outproj_kernel(att_ref, conv_ref, x_ref, g1_ref, sh_ref, sc_ref, ga_ref, gc_ref, woa_ref, woc_ref,
                    g2_ref, wrh_ref, wrl_ref, br_ref, x1_ref, h2_ref, gates_ref):
    a = (_rms(att_ref[...]) * ga_ref[...]).astype(BF16)
    c = (_rms(conv_ref[...]) * gc_ref[...]).astype(BF16)
    mixed = _dot(a, woa_ref[...]) + _dot(c, woc_ref[...])
    x1 = x_ref[...] + g1_ref[0] * mixed
    x1_ref[...] = x1
    h2 = (_rms(x1) * g2_ref[...]) * (1.0 + sc_ref[0]) + sh_ref[0]
    hi = h2.astype(BF16)
    h2_ref[...] = hi
    lo = (h2 - hi.astype(F32)).astype(BF16)
    logits = _dot(hi, wrh_ref[...]) + _dot(lo, wrh_ref[...]) + _dot(hi, wrl_ref[...]) + br_ref[...]
    lane = lax.broadcasted_iota(I32, logits.shape, 1).astype(F32)
    neg_inf = -jnp.inf
    far = float(4 * LANES)
    is_g = jnp.logical_and(lane >= N_EXPERTS, lane < N_EXPERTS + N_GROUPS)
    gl = jnp.where(is_g, logits, neg_inf)
    gmax = jnp.max(gl, axis=-1, keepdims=True)
    gsel = jnp.min(jnp.where(gl == gmax, lane, far), axis=-1, keepdims=True) - N_EXPERTS
    p_g = 1.0 / jnp.sum(jnp.exp(gl - gmax), axis=-1, keepdims=True)
    in_grp = jnp.logical_and(lane >= gsel * EXPERTS_PER_GROUP, lane < (gsel + 1.0) * EXPERTS_PER_GROUP)
    el = jnp.where(in_grp, logits, neg_inf)
    v1 = jnp.max(el, axis=-1, keepdims=True)
    i1 = jnp.min(jnp.where(el == v1, lane, far), axis=-1, keepdims=True)
    el2 = jnp.where(lane == i1, neg_inf, el)
    v2 = jnp.max(el2, axis=-1, keepdims=True)
    i2 = jnp.min(jnp.where(el2 == v2, lane, far), axis=-1, keepdims=True)
    e2 = jnp.exp(v2 - v1)
    den = 1.0 + e2
    gates_ref[...] = (jnp.where(lane == i1, (1.0 / den) * p_g, 0.0)
                      + jnp.where(lane == i2, (e2 / den) * p_g, 0.0))


def _outproj(att, conv, x, gate1, shift2, scale2, ga, gc, woa, woc, g2, wrh, wrl, br, tm, tiles_per_group):
    n, d = x.shape
    rows = gate1.shape[1]
    full = lambda a: pl.BlockSpec(a.shape, lambda i: (0,) * a.ndim)
    row = lambda w: pl.BlockSpec((tm, w), lambda i: (i, 0))
    ms = _mod_spec(rows, d, tiles_per_group)
    return pl.pallas_call(
        _outproj_kernel,
        grid=(n // tm,),
        in_specs=[row(att.shape[1]), row(conv.shape[1]), row(d), ms, ms, ms,
                  full(ga), full(gc), full(woa), full(woc), full(g2), full(wrh), full(wrl), full(br)],
        out_specs=[row(d), row(d), row(LANES)],
        out_shape=[jax.ShapeDtypeStruct((n, d), F32), jax.ShapeDtypeStruct((n, d), BF16),
                   jax.ShapeDtypeStruct((n, LANES), F32)],
        compiler_params=_cparams(("parallel",)),
        name="outproj_router",
    )(att, conv, x, gate1, shift2, scale2, ga, gc, woa, woc, g2, wrh, wrl, br)


def _moe_kernel(h_ref, gates_ref, w1_ref, w3_ref, w2_ref, x1_ref, g2_ref, gf_ref, y_ref, acc_ref):
    e = pl.program_id(1)

    @pl.when(e == 0)
    def _():
        acc_ref[...] = jnp.zeros(acc_ref.shape, F32)

    gates = gates_ref[...]
    lane = lax.broadcasted_iota(I32, gates.shape, 1)
    gcol = jnp.sum(jnp.where(lane == e, gates, 0.0), axis=-1, keepdims=True)

    hb = h_ref[...]
    a = _dot(hb, w1_ref[0])
    b = _dot(hb, w3_ref[0])
    hid = (a * jax.nn.sigmoid(a)) * b
    acc_ref[...] += gcol * _dot(hid.astype(BF16), w2_ref[0])

    @pl.when(e == pl.num_programs(1) - 1)
    def _():
        xf = x1_ref[...] + g2_ref[0] * acc_ref[...]
        y_ref[...] = _rms(xf) * gf_ref[...]


def _moe(h2, gates, w1, w3, w2, x1, gate2, g_final, tm, tiles_per_group):
    n, d = x1.shape
    n_e, _, f = w1.shape
    rows = gate2.shape[1]
    row = lambda w: pl.BlockSpec((tm, w), lambda i, e: (i, 0))
    return pl.pallas_call(
        _moe_kernel,
        grid=(n // tm, n_e),
        in_specs=[row(d), row(LANES),
                  pl.BlockSpec((1, d, f), lambda i, e: (e, 0, 0)),
                  pl.BlockSpec((1, d, f), lambda i, e: (e, 0, 0)),
                  pl.BlockSpec((1, f, d), lambda i, e: (e, 0, 0)),
                  row(d),
                  pl.BlockSpec((1, rows, d), lambda i, e: (i // tiles_per_group, 0, 0)),
                  pl.BlockSpec((1, d), lambda i, e: (0, 0))],
        out_specs=row(d),
        out_shape=jax.ShapeDtypeStruct((n, d), F32),
        scratch_shapes=[pltpu.VMEM((tm, d), F32)],
        compiler_params=_cparams(("parallel", "arbitrary")),
        name="moe_final",
    )(h2, gates, w1, w3, w2, x1, gate2, g_final)


def _tile(n, pref):
    t = min(n, pref)
    assert n % t == 0, (n, pref)
    return t


def _prep_weights(w_in, w_out, w_rg, b_rg, w_re, b_re, w1, w3, w2):
    d = w_in.shape[0]
    off_qi = 3 * ATT_WIDTH
    off_ki = off_qi + IDX_HEADS * IDX_DIM
    off_conv = off_ki + IDX_DIM + IDX_HEADS
    cc = (w_in.shape[1] - off_conv) // 2
    wb = w_in.astype(BF16)
    wkw = jnp.zeros((d, LANES), BF16).at[:, :IDX_DIM + IDX_HEADS].set(wb[:, off_ki:off_conv])
    in_w = (wb[:, :off_qi], wb[:, off_qi:off_ki], wkw, wb[:, off_conv:off_conv + cc], wb[:, off_conv + cc:])
    wob = w_out.astype(BF16)
    out_w = (wob[:ATT_WIDTH], wob[ATT_WIDTH:])
    wr = jnp.zeros((d, LANES), F32)
    wr = wr.at[:, :N_EXPERTS].set(jnp.transpose(w_re, (1, 0, 2)).reshape(d, N_EXPERTS))
    wr = wr.at[:, N_EXPERTS:N_EXPERTS + N_GROUPS].set(w_rg)
    br = jnp.zeros((1, LANES), F32).at[0, :N_EXPERTS].set(b_re.reshape(-1))
    br = br.at[0, N_EXPERTS:N_EXPERTS + N_GROUPS].set(b_rg)
    wrh = wr.astype(BF16)
    wrl = (wr - wrh.astype(F32)).astype(BF16)
    return in_w, out_w, (wrh, wrl, br), (w1.astype(BF16), w3.astype(BF16), w2.astype(BF16))


def _token_layers(x2, mod_rows, tm, tm_moe, attn_fn, conv_fn, norms, weights):
    g1, ga, gc, g2, gf = norms
    in_w, out_w, router_w, expert_w = weights
    (shift1, tpg), (scale1, _), (gate1, _), (shift2, _), (scale2, _) = [mod_rows(j, tm) for j in range(5)]
    gate2, tpg_moe = mod_rows(5, tm_moe)
    q, k, v, kb, vb, qi, kw, kib, glu = _inproj(x2, shift1, scale1, g1, in_w, tm, tpg)
    att = attn_fn(q, k, v, kb, vb, qi, kw, kib)
    conv = conv_fn(glu)
    x1, h2, gates = _outproj(att, conv, x2, gate1, shift2, scale2, ga, gc, out_w[0], out_w[1], g2,
                             *router_w, tm, tpg)
    y = _moe(h2, gates, *expert_w, x1, gate2, gf, tm_moe, tpg_moe)
    return y, k, v, kw, glu


def kernel(x_prompt, x_sample, c_prompt, c_sample, cache_k, cache_v, cache_kidx, state_conv, page_table,
           w_ada, b_ada, g_norm1, w_in, conv_w, conv_b, conv_ln_g, conv_ln_b, g_attn_out, g_conv_out,
           w_out, g_norm2, w_rg, b_rg, w_re, b_re, w1, w3, w2, g_final):
    bp, s, d = x_prompt.shape
    bs, t_new, _ = x_sample.shape
    cc = conv_w.shape[1]
    n_pool, page = cache_k.shape[:2]
    hist = CONV_WIDTH - 1

    weights = _prep_weights(w_in, w_out, w_rg, b_rg, w_re, b_re, w1, w3, w2)
    norms = (g_norm1.reshape(1, d), g_attn_out.reshape(1, ATT_WIDTH), g_conv_out.reshape(1, cc),
             g_norm2.reshape(1, d), g_final.reshape(1, d))

    n_c = bp + bs
    pad = (-n_c) % 8
    c_all = jnp.concatenate([c_prompt, c_sample, jnp.zeros((pad, d), F32)], axis=0)
    mod = _modulation(c_all, w_ada, b_ada).reshape(n_c + pad, 6, d)

    tm_p = _tile(s, 256)

    def mod_p(j, tile):
        return mod[:bp, j].reshape(bp, 1, d), s // tile

    def prompt_attn(q, k, v, kb, vb, qi, kw, kib):
        r3 = lambda a: a.reshape(bp, s, a.shape[-1])
        tq = _tile(s, 256)
        tk = _tile(s, 512)
        return _prompt_attention(r3(q), r3(qi), r3(kw), r3(kb), r3(vb), r3(kib), tq, tk).reshape(bp * s, ATT_WIDTH)

    def prompt_conv(glu):
        prefix = jnp.zeros((bp, HALO, cc), F32)
        tt = _tile(s, 512)
        return _conv_module(glu.reshape(bp, s, cc), prefix, conv_w, conv_b, conv_ln_g, conv_ln_b, tt).reshape(bp * s, cc)

    y_p, k_p, v_p, kw_p, glu_p = _token_layers(
        x_prompt.reshape(bp * s, d), mod_p, tm_p, _tile(s, 1024), prompt_attn, prompt_conv, norms, weights)

    kidx_t = jnp.transpose(cache_kidx, (0, 2, 1))
    k_t = jnp.transpose(cache_k, (0, 2, 3, 1)).reshape(n_pool, ATT_WIDTH, page)
    v_t = jnp.transpose(cache_v, (0, 2, 3, 1)).reshape(n_pool, ATT_WIDTH, page)
    n_s = bs * t_new
    tm_s = _tile(n_s, 256)
    mod_tok = jnp.repeat(mod[bp:bp + bs], t_new, axis=0)

    def mod_s(j, tile):
        return mod_tok[:, j].reshape(n_s // tile, tile, d), 1

    def sample_attn(q, k, v, kb, vb, qi, kw, kib):
        tp = 8 - t_new
        qi_rows = jnp.pad(qi.reshape(bs, t_new, IDX_HEADS, IDX_DIM).transpose(0, 2, 1, 3),
                          ((0, 0), (0, 0), (0, tp), (0, 0))).reshape(bs, 64, IDX_DIM)
        wi = kw[:, IDX_DIM:IDX_DIM + IDX_HEADS].reshape(bs, t_new, IDX_HEADS).transpose(0, 2, 1)
        w_rows = jnp.broadcast_to(jnp.pad(wi, ((0, 0), (0, 0), (0, tp))).reshape(bs, 64, 1), (bs, 64, LANES))
        rows_pad = lambda a: jnp.pad(a.reshape(bs, t_new, a.shape[-1]), ((0, 0), (0, LANES - t_new), (0, 0)))
        scores = _sample_scores(page_table, kidx_t, qi_rows, w_rows, rows_pad(kib), t_new)
        topk = min(TOPK_MAX, (page_table.shape[1] * page + t_new) // 4)
        bias = _sample_select(scores, topk, _tile(bs, SELECT_GROUP))
        qh = jnp.pad(q.reshape(bs, t_new, N_HEADS, HEAD_DIM).transpose(0, 2, 1, 3),
                     ((0, 0), (0, 0), (0, tp), (0, 0)))
        eye = jnp.eye(N_HEADS, dtype=q.dtype)
        q_bd = (qh[:, :, :, None, :] * eye[None, :, None, :, None]).reshape(bs, 64, ATT_WIDTH)
        o = _sample_attention(page_table, k_t, v_t, q_bd, bias, rows_pad(kb), rows_pad(vb))
        return o[:, :t_new].reshape(n_s, ATT_WIDTH)

    def sample_conv(glu):
        prefix = jnp.pad(state_conv, ((0, 0), (HALO - hist, 0), (0, 0)))
        glu8 = jnp.pad(glu.reshape(bs, t_new, cc), ((0, 0), (0, 8 - t_new), (0, 0)))
        return _conv_module(glu8, prefix, conv_w, conv_b, conv_ln_g, conv_ln_b, 8)[:, :t_new].reshape(n_s, cc)

    y_s, k_s, v_s, kw_s, glu_s = _token_layers(
        x_sample.reshape(n_s, d), mod_s, tm_s, _tile(n_s, 1024), sample_attn, sample_conv, norms, weights)

    glu_p3 = glu_p.reshape(bp, s, cc)
    conv_prompt = glu_p3[:, s - hist:]
    conv_sample = jnp.concatenate([state_conv, glu_s.reshape(bs, t_new, cc)], axis=1)[:, -hist:]
    return (y_p.reshape(bp, s, d), y_s.reshape(bs, t_new, d),
            k_p.reshape(bp, s, N_HEADS, HEAD_DIM), v_p.reshape(bp, s, N_HEADS, HEAD_DIM),
            kw_p[:, :IDX_DIM].reshape(bp, s, IDX_DIM), conv_prompt,
            k_s.reshape(bs, t_new, N_HEADS, HEAD_DIM), v_s.reshape(bs, t_new, N_HEADS, HEAD_DIM),
            kw_s[:, :IDX_DIM].reshape(bs, t_new, IDX_DIM), conv_sample)
```

```python
import functools

import jax
import jax.numpy as jnp
from jax import lax
from jax.experimental import pallas as pl
from jax.experimental.pallas import tpu as pltpu

F32 = jnp.float32
BF16 = jnp.bfloat16
I32 = jnp.int32

EPS = 1e-6
HEAD_DIM = 64
N_HEADS = 8
ATT_WIDTH = N_HEADS * HEAD_DIM
IDX_HEADS = 8
IDX_DIM = 64
CONV_WIDTH = 31
TOPK_MAX = 256
N_GROUPS = 4
EXPERTS_PER_GROUP = 8
N_EXPERTS = N_GROUPS * EXPERTS_PER_GROUP
LANES = 128
HALO = 32
PAGES_PER_STEP = 16
SELECT_GROUP = 16
SELECT_ROWS = 128
PAIRS_PER_LOOP = 2
MOE_EXPERTS_PER_STEP = 4
NEG_BIG = -1e30
MAX_BISECT = 24
KEEP_ALL = 1e9
VMEM_LIMIT = 56 * 1024 * 1024


def _cparams(sem):
    return pltpu.CompilerParams(dimension_semantics=sem, vmem_limit_bytes=VMEM_LIMIT)


def _rms(x):
    return x * lax.rsqrt(jnp.mean(x * x, axis=-1, keepdims=True) + EPS)


def _dot(a, b):
    return jnp.dot(a, b, preferred_element_type=F32)


def _dot_nt(a, b):
    return lax.dot_general(a, b, (((1,), (1,)), ((), ())), preferred_element_type=F32)


def _mod_kernel(c_ref, w_ref, b_ref, o_ref):
    c = c_ref[...]
    a = (c * jax.nn.sigmoid(c)).astype(BF16)
    o_ref[...] = _dot(a, w_ref[...].astype(BF16)) + b_ref[...]


def _modulation(c, w_ada, b_ada, tn=512):
    r, d = c.shape
    n = w_ada.shape[1]
    return pl.pallas_call(
        _mod_kernel,
        grid=(n // tn,),
        in_specs=[pl.BlockSpec((r, d), lambda j: (0, 0)),
                  pl.BlockSpec((d, tn), lambda j: (0, j)),
                  pl.BlockSpec((1, tn), lambda j: (0, j))],
        out_specs=pl.BlockSpec((r, tn), lambda j: (0, j)),
        out_shape=jax.ShapeDtypeStruct((r, n), F32),
        compiler_params=_cparams(("arbitrary",)),
        name="adaln_mod",
    )(c, w_ada, b_ada.reshape(1, n))


def _inproj_kernel(x_ref, sh_ref, sc_ref, g_ref, wqkv_ref, wqi_ref, wkw_ref, wca_ref, wcb_ref,
                   q_ref, k_ref, v_ref, kb_ref, vb_ref, qi_ref, kw_ref, kib_ref, glu_ref):
    h = _rms(x_ref[...]) * g_ref[...]
    h = h * (1.0 + sc_ref[0]) + sh_ref[0]
    hb = h.astype(BF16)
    qkv = _dot(hb, wqkv_ref[...])
    q_ref[...] = (qkv[:, :ATT_WIDTH] * (HEAD_DIM ** -0.5)).astype(BF16)
    k = qkv[:, ATT_WIDTH:2 * ATT_WIDTH]
    v = qkv[:, 2 * ATT_WIDTH:]
    k_ref[...] = k
    v_ref[...] = v
    kb_ref[...] = k.astype(BF16)
    vb_ref[...] = v.astype(BF16)
    qi_ref[...] = _dot(hb, wqi_ref[...]).astype(BF16)
    kw = _dot(hb, wkw_ref[...])
    kw_ref[...] = kw
    kib_ref[...] = kw[:, :IDX_DIM].astype(BF16)
    a = _dot(hb, wca_ref[...])
    b = _dot(hb, wcb_ref[...])
    glu_ref[...] = a * jax.nn.sigmoid(b)


def _mod_spec(rows, d, tiles_per_group):
    return pl.BlockSpec((1, rows, d), lambda i: (i // tiles_per_group, 0, 0))


def _inproj(x, shift, scale, g1, wts, tm, tiles_per_group):
    n, d = x.shape
    wqkv, wqi, wkw, wca, wcb = wts
    cc = wca.shape[1]
    rows = shift.shape[1]
    full = lambda a: pl.BlockSpec(a.shape, lambda i: (0,) * a.ndim)
    row = lambda w: pl.BlockSpec((tm, w), lambda i: (i, 0))
    outs = [
        (ATT_WIDTH, BF16),
        (ATT_WIDTH, F32),
        (ATT_WIDTH, F32),
        (ATT_WIDTH, BF16),
        (ATT_WIDTH, BF16),
        (IDX_HEADS * IDX_DIM, BF16),
        (LANES, F32),
        (IDX_DIM, BF16),
        (cc, F32),
    ]
    return pl.pallas_call(
        _inproj_kernel,
        grid=(n // tm,),
        in_specs=[row(d), _mod_spec(rows, d, tiles_per_group), _mod_spec(rows, d, tiles_per_group),
                  full(g1), full(wqkv), full(wqi), full(wkw), full(wca), full(wcb)],
        out_specs=[row(w) for w, _ in outs],
        out_shape=[jax.ShapeDtypeStruct((n, w), dt) for w, dt in outs],
        compiler_params=_cparams(("parallel",)),
        name="inproj",
    )(x, shift, scale, g1, wqkv, wqi, wkw, wca, wcb)


def _fold(scan, rows, init, fn, red):
    def body(sc, part, sl=slice(None)):
        for g in range(sc.shape[-1] // LANES):
            part = fn(part, sc[..., g * LANES:(g + 1) * LANES], sl)
        return part
    return red(scan(body, jnp.full(rows + (LANES,), init, F32)), axis=-1, keepdims=True)


def _kth_select(scan, rows, k):
    inf = jnp.inf
    count_ge = lambda t: _fold(scan, rows, 0.0, lambda p, x, sl: p + jnp.where(x >= t[sl], 1.0, 0.0), jnp.sum)
    count_gt = lambda t: _fold(scan, rows, 0.0, lambda p, x, sl: p + jnp.where(x > t[sl], 1.0, 0.0), jnp.sum)
    min_ge = lambda t: _fold(scan, rows, inf,
                             lambda p, x, sl: jnp.minimum(p, jnp.where(x >= t[sl], x, inf)), jnp.min)
    min_gt = lambda t: _fold(scan, rows, inf,
                             lambda p, x, sl: jnp.minimum(p, jnp.where(x > t[sl], x, inf)), jnp.min)
    row_max = _fold(scan, rows, -inf, lambda p, x, sl: jnp.maximum(p, x), jnp.max)
    row_min = min_gt(jnp.full(rows + (1,), -inf, F32))
    n_valid = count_ge(row_min)
    any_row = lambda flag: jnp.max(flag) > 0.5

    def bisect(lo, hi, cnt, pending):
        def cond(st):
            it, _, _, _, act = st
            return jnp.logical_and(it < MAX_BISECT, any_row(act))

        def body(st):
            it, lo, hi, cnt, act = st
            mid = 0.5 * (lo + hi)
            c = count_ge(mid)
            go_up = jnp.where(c >= k, act, 0.0) > 0.5
            go_dn = jnp.where(c >= k, 0.0, act) > 0.5
            lo2 = jnp.where(go_up, mid, lo)
            cnt2 = jnp.where(go_up, c, cnt)
            hi2 = jnp.where(go_dn, mid, hi)
            moving = jnp.logical_and(jnp.logical_and(mid > lo, mid < hi), cnt2 > k)
            return it + 1, lo2, hi2, cnt2, jnp.where(moving, act, 0.0)

        act0 = jnp.where(cnt > k, pending, 0.0)
        _, lo, hi, cnt, _ = lax.while_loop(cond, body, (jnp.int32(0), lo, hi, cnt, act0))
        return lo, hi, cnt

    def outer_cond(st):
        return any_row(st[5])

    def outer_body(st):
        lo, hi, cnt, thr, need, pending = st
        lo, hi, cnt = bisect(lo, hi, cnt, pending)
        plain = jnp.where(cnt <= k, pending, 0.0)
        hard = jnp.where(cnt <= k, 0.0, pending)
        thr = jnp.where(plain > 0.5, lo, thr)

        def resolve(_):
            t = min_ge(lo)
            n_gt = count_gt(t)
            tied = jnp.where(n_gt < k, hard, 0.0) > 0.5
            above_f = jnp.where(n_gt < k, 0.0, hard)
            above = above_f > 0.5
            return (jnp.where(above, min_gt(t), lo), jnp.where(above, n_gt, cnt),
                    jnp.where(tied, t, thr), jnp.where(tied, k - n_gt, need), above_f)

        def settled(_):
            return lo, cnt, thr, need, jnp.zeros(rows + (1,), F32)

        lo, cnt, thr, need, pending = lax.cond(any_row(hard), resolve, settled, 0)
        return lo, hi, cnt, thr, need, pending

    lo0 = row_min
    hi0 = row_max + (row_max - row_min) + 1.0
    st0 = (lo0, hi0, n_valid, lo0, jnp.full(rows + (1,), KEEP_ALL, F32), jnp.ones(rows + (1,), F32))
    _, _, _, thr, need, _ = lax.while_loop(outer_cond, outer_body, st0)
    return thr, need


def _prompt_attn_kernel(q_ref, qi_ref, kw_ref, kb_ref, vb_ref, kib_ref, o_ref,
                        score_ref, bias_ref, *, tq, tk, topk):
    i = pl.program_id(1)
    nk = ((i + 1) * tq + tk - 1) // tk
    lane = lax.broadcasted_iota(I32, (tq, tk), 1)
    qpos = i * tq + lax.broadcasted_iota(I32, (tq, tk), 0)
    wi = kw_ref[0][:, IDX_DIM:IDX_DIM + IDX_HEADS]
    qi = qi_ref[0]

    def score_chunk(c, carry):
        kic = kib_ref[0, pl.ds(pl.multiple_of(c * tk, tk), tk), :]
        acc = jnp.zeros((tq, tk), F32)
        for h in range(IDX_HEADS):
            s = _dot_nt(qi[:, h * IDX_DIM:(h + 1) * IDX_DIM], kic)
            acc = acc + wi[:, h:h + 1] * jnp.maximum(s, 0.0)
        valid = (c * tk + lane) <= qpos
        score_ref[c] = jnp.where(valid, acc, -jnp.inf)
        return carry
    lax.fori_loop(0, nk, score_chunk, 0)

    def scan(body, init):
        outs = []
        for r in range(0, tq, SELECT_ROWS):
            sl = slice(r, min(r + SELECT_ROWS, tq))
            outs.append(lax.fori_loop(0, nk, lambda c, part, sl=sl: body(score_ref[c, sl, :], part, sl),
                                      init[sl]))
        return jnp.concatenate(outs, axis=0)
    thr, need = _kth_select(scan, (tq,), topk)

    def bias_chunk(c, carry):
        bias_ref[c] = jnp.where(score_ref[c] >= thr, 0.0, NEG_BIG)
        return carry
    lax.fori_loop(0, nk, bias_chunk, 0)

    tie_f = jnp.where(need < KEEP_ALL, 1.0, 0.0)

    @pl.when(jnp.max(tie_f) > 0.5)
    def _():
        upper = (lax.broadcasted_iota(I32, (tk, tk), 0)
                 < lax.broadcasted_iota(I32, (tk, tk), 1)).astype(BF16)

        def tie_chunk(c, before):
            sc = score_ref[c]
            eq = sc == thr
            eqf = jnp.where(eq, 1.0, 0.0)
            rank = before + _dot(eqf.astype(BF16), upper)
            tie_bias = jnp.where(jnp.logical_or(sc > thr, jnp.logical_and(eq, rank < need)), 0.0, NEG_BIG)
            bias_ref[c] = jnp.where(tie_f > 0.5, tie_bias, bias_ref[c])
            return before + jnp.sum(eqf, axis=-1, keepdims=True)
        lax.fori_loop(0, nk, tie_chunk, jnp.zeros((tq, 1), F32))

    half = lax.broadcasted_iota(I32, (tq, LANES), 1) < HEAD_DIM
    for p0 in range(0, N_HEADS // 2, PAIRS_PER_LOOP):
        slabs = [slice(p * LANES, (p + 1) * LANES) for p in range(p0, p0 + PAIRS_PER_LOOP)]
        qs = []
        for sl in slabs:
            qp = q_ref[0, :, sl]
            qs.append((jnp.where(half, qp, jnp.zeros_like(qp)), jnp.where(half, jnp.zeros_like(qp), qp)))

        def attn_chunk(c, carry):
            off = pl.multiple_of(c * tk, tk)
            bias = bias_ref[c]
            new = []
            for n, sl in enumerate(slabs):
                kc = kb_ref[0, pl.ds(off, tk), sl]
                vc = vb_ref[0, pl.ds(off, tk), sl]
                for e in range(2):
                    m, l, acc = carry[2 * n + e]
                    s = _dot_nt(qs[n][e], kc) + bias
                    m_new = jnp.maximum(m, jnp.max(s, axis=-1, keepdims=True))
                    alpha = jnp.exp(m - m_new)
                    pr = jnp.exp(s - m_new)
                    l = alpha * l + jnp.sum(pr, axis=-1, keepdims=True)
                    acc = alpha * acc + _dot(pr.astype(BF16), vc)
                    new.append((m_new, l, acc))
            return tuple(new)

        init = tuple((jnp.full((tq, 1), NEG_BIG, F32), jnp.zeros((tq, 1), F32),
                      jnp.zeros((tq, LANES), F32)) for _ in range(2 * PAIRS_PER_LOOP))
        res = lax.fori_loop(0, nk, attn_chunk, init)
        for n, sl in enumerate(slabs):
            (_, l0, a0), (_, l1, a1) = res[2 * n], res[2 * n + 1]
            o_ref[0, :, sl] = jnp.where(half, a0 / l0, a1 / l1)


def _prompt_attention(q, qi, kw, kb, vb, kib, tq, tk):
    b, s, _ = q.shape
    topk = min(TOPK_MAX, s // 4)
    blk = lambda w: pl.BlockSpec((1, tq, w), lambda bi, i: (bi, i, 0))
    seq = lambda w: pl.BlockSpec((1, s, w), lambda bi, i: (bi, 0, 0))
    return pl.pallas_call(
        functools.partial(_prompt_attn_kernel, tq=tq, tk=tk, topk=topk),
        grid=(b, s // tq),
        in_specs=[blk(ATT_WIDTH), blk(IDX_HEADS * IDX_DIM), blk(LANES),
                  seq(ATT_WIDTH), seq(ATT_WIDTH), seq(IDX_DIM)],
        out_specs=blk(ATT_WIDTH),
        out_shape=jax.ShapeDtypeStruct((b, s, ATT_WIDTH), F32),
        scratch_shapes=[pltpu.VMEM((s // tk, tq, tk), F32), pltpu.VMEM((s // tk, tq, tk), F32)],
        compiler_params=_cparams(("parallel", "arbitrary")),
        name="prompt_dsa_attention",
    )(q, qi, kw, kb, vb, kib)


def _sample_scores_kernel(pt_ref, *refs, n_pages, t_new):
    del pt_ref
    pages = refs[:n_pages]
    qi_ref, w_ref, kin_ref, score_ref = refs[n_pages:]
    qi = qi_ref[0]
    w = w_ref[0]

    def combine(s):
        acc = jnp.zeros((8, LANES), F32)
        for h in range(IDX_HEADS):
            acc = acc + w[h * 8:(h + 1) * 8] * jnp.maximum(s[h * 8:(h + 1) * 8], 0.0)
        return acc

    for j in range(n_pages):
        page_t = pages[j][...].astype(BF16)
        score_ref[j] = combine(_dot(qi, page_t))

    s_new = combine(_dot_nt(qi, kin_ref[0]))
    lane = lax.broadcasted_iota(I32, (8, LANES), 1)
    row = lax.broadcasted_iota(I32, (8, LANES), 0)
    valid = jnp.logical_and(lane < t_new, lane <= row)
    score_ref[n_pages] = jnp.where(valid, s_new, -jnp.inf)


def _sample_scores(page_table, kidx_t, qi_rows, w_rows, ki_new, t_new):
    b, n_pages = page_table.shape
    page = kidx_t.shape[2]

    def page_spec(j):
        return pl.BlockSpec((None, IDX_DIM, page), lambda bi, pt: (pt[bi, j], 0, 0))
    per_seq = lambda shp: pl.BlockSpec((1,) + shp, lambda bi, pt: (bi,) + (0,) * len(shp))
    grid_spec = pltpu.PrefetchScalarGridSpec(
        num_scalar_prefetch=1,
        grid=(b,),
        in_specs=[page_spec(j) for j in range(n_pages)]
                 + [per_seq((64, IDX_DIM)), per_seq((64, LANES)), per_seq((LANES, IDX_DIM))],
        out_specs=pl.BlockSpec((n_pages + 1, None, 8, LANES), lambda bi, pt: (0, bi, 0, 0)),
    )
    return pl.pallas_call(
        functools.partial(_sample_scores_kernel, n_pages=n_pages, t_new=t_new),
        grid_spec=grid_spec,
        out_shape=jax.ShapeDtypeStruct((n_pages + 1, b, 8, LANES), F32),
        compiler_params=_cparams(("parallel",)),
        name="sample_dsa_scores",
    )(page_table, *([kidx_t] * n_pages), qi_rows, w_rows, ki_new)


def _sample_select_kernel(score_ref, bias_ref, *, topk):
    n_chunks, g = score_ref.shape[:2]
    unroll = max(u for u in range(1, 9) if n_chunks % u == 0)

    def scan(body, init):
        def step(c, part):
            for u in range(unroll):
                part = body(score_ref[c * unroll + u], part)
            return part
        return lax.fori_loop(0, n_chunks // unroll, step, init)
    thr, need = _kth_select(scan, (g, 8), topk)

    def bias_page(c, carry):
        bias_ref[c] = jnp.where(score_ref[c] >= thr, 0.0, NEG_BIG)
        return carry
    lax.fori_loop(0, n_chunks, bias_page, 0)

    tie_f = jnp.where(need < KEEP_ALL, 1.0, 0.0)

    @pl.when(jnp.max(tie_f) > 0.5)
    def _():
        upper = (lax.broadcasted_iota(I32, (LANES, LANES), 0)
                 < lax.broadcasted_iota(I32, (LANES, LANES), 1)).astype(BF16)

        def tie_page(c, before):
            sc = score_ref[c]
            eq = sc == thr
            eqf = jnp.where(eq, 1.0, 0.0)
            low = _dot(eqf.reshape(g * 8, LANES).astype(BF16), upper).reshape(g, 8, LANES)
            rank = before + low
            tie_bias = jnp.where(jnp.logical_or(sc > thr, jnp.logical_and(eq, rank < need)), 0.0, NEG_BIG)
            bias_ref[c] = jnp.where(tie_f > 0.5, tie_bias, bias_ref[c])
            return before + jnp.sum(eqf, axis=-1, keepdims=True)
        lax.fori_loop(0, n_chunks, tie_page, jnp.zeros((g, 8, 1), F32))


def _sample_select(scores, topk, group):
    n_chunks, b = scores.shape[:2]
    spec = pl.BlockSpec((n_chunks, group, 8, LANES), lambda i: (0, i, 0, 0))
    return pl.pallas_call(
        functools.partial(_sample_select_kernel, topk=topk),
        grid=(b // group,),
        in_specs=[spec],
        out_specs=spec,
        out_shape=jax.ShapeDtypeStruct(scores.shape, F32),
        compiler_params=_cparams(("parallel",)),
        name="sample_dsa_select",
    )(scores)


def _sample_attn_kernel(pt_ref, *refs, n_steps):
    del pt_ref
    kpages = refs[:PAGES_PER_STEP]
    vpages = refs[PAGES_PER_STEP:2 * PAGES_PER_STEP]
    (q_ref, bias_ref, bias_new_ref, kn_ref, vn_ref, o_ref,
     kt_ref, vt_ref, m_ref, l_ref, acc_ref) = refs[2 * PAGES_PER_STEP:]
    g = pl.program_id(1)
    q = q_ref[0]

    @pl.when(g == 0)
    def _():
        m_ref[...] = jnp.full(m_ref.shape, NEG_BIG, F32)
        l_ref[...] = jnp.zeros(l_ref.shape, F32)
        acc_ref[...] = jnp.zeros(acc_ref.shape, F32)

    def update(s, pv):
        m = m_ref[...]
        m_new = jnp.maximum(m, jnp.max(s, axis=-1, keepdims=True))
        alpha = jnp.exp(m - m_new)
        pr = jnp.exp(s - m_new)
        l_ref[...] = alpha * l_ref[...] + jnp.sum(pr, axis=-1, keepdims=True)
        acc_ref[...] = alpha * acc_ref[...] + pv(pr.astype(BF16))
        m_ref[...] = m_new

    tile8 = lambda b8: jnp.concatenate([b8] * N_HEADS, axis=0)

    for j in range(PAGES_PER_STEP):
        kt_ref[:, j * LANES:(j + 1) * LANES] = kpages[j][...].astype(BF16)
        vt_ref[:, j * LANES:(j + 1) * LANES] = vpages[j][...].astype(BF16)
    s = _dot(q, kt_ref[...])
    s = jnp.concatenate([s[:, j * LANES:(j + 1) * LANES] + tile8(bias_ref[j])
                         for j in range(PAGES_PER_STEP)], axis=1)
    update(s, lambda pr: _dot_nt(pr, vt_ref[...]))

    @pl.when(g == n_steps - 1)
    def _():
        update(_dot_nt(q, kn_ref[0]) + tile8(bias_new_ref[0]), lambda pr: _dot(pr, vn_ref[0]))
        o = acc_ref[...] / l_ref[...]
        lane = lax.broadcasted_iota(I32, (8, ATT_WIDTH), 1)
        out = jnp.zeros((8, ATT_WIDTH), F32)
        for h in range(N_HEADS):
            in_head = jnp.logical_and(lane >= h * HEAD_DIM, lane < (h + 1) * HEAD_DIM)
            out = out + jnp.where(in_head, o[h * 8:(h + 1) * 8], 0.0)
        o_ref[0] = out


def _sample_attention(page_table, k_t, v_t, q_bd, bias, k_new, v_new):
    b, n_pages = page_table.shape
    page = k_t.shape[2]
    n_steps = n_pages // PAGES_PER_STEP

    def page_spec(j):
        return pl.BlockSpec((None, ATT_WIDTH, page),
                            lambda bi, g, pt: (pt[bi, g * PAGES_PER_STEP + j], 0, 0))
    per_seq = lambda shp: pl.BlockSpec((1,) + shp, lambda bi, g, pt: (bi,) + (0,) * len(shp))
    grid_spec = pltpu.PrefetchScalarGridSpec(
        num_scalar_prefetch=1,
        grid=(b, n_steps),
        in_specs=[page_spec(j) for j in range(PAGES_PER_STEP)] * 2
                 + [per_seq((64, ATT_WIDTH)),
                    pl.BlockSpec((PAGES_PER_STEP, None, 8, LANES), lambda bi, g, pt: (g, bi, 0, 0)),
                    pl.BlockSpec((1, None, 8, LANES), lambda bi, g, pt: (n_pages, bi, 0, 0)),
                    per_seq((LANES, ATT_WIDTH)), per_seq((LANES, ATT_WIDTH))],
        out_specs=per_seq((8, ATT_WIDTH)),
        scratch_shapes=[pltpu.VMEM((ATT_WIDTH, PAGES_PER_STEP * page), BF16),
                        pltpu.VMEM((ATT_WIDTH, PAGES_PER_STEP * page), BF16),
                        pltpu.VMEM((64, 1), F32), pltpu.VMEM((64, 1), F32),
                        pltpu.VMEM((64, ATT_WIDTH), F32)],
    )
    return pl.pallas_call(
        functools.partial(_sample_attn_kernel, n_steps=n_steps),
        grid_spec=grid_spec,
        out_shape=jax.ShapeDtypeStruct((b, 8, ATT_WIDTH), F32),
        compiler_params=_cparams(("parallel", "arbitrary")),
        name="sample_dsa_attention",
    )(page_table, *([k_t] * PAGES_PER_STEP), *([v_t] * PAGES_PER_STEP),
      q_bd, bias, bias, k_new, v_new)


def _conv_kernel(cur_ref, prev_ref, pre_ref, cw_ref, cb_ref, g_ref, b_ref, o_ref, win_ref, *, tt):
    i = pl.program_id(1)

    @pl.when(i == 0)
    def _():
        win_ref[0:HALO] = pre_ref[0]

    if tt >= HALO:
        @pl.when(i > 0)
        def _():
            win_ref[0:HALO] = prev_ref[0, tt - HALO:tt, :]

    win_ref[HALO:HALO + tt] = cur_ref[0]
    acc = jnp.zeros(o_ref.shape[1:], F32) + cb_ref[...]
    for j in range(CONV_WIDTH):
        acc = acc + win_ref[pl.ds(j + HALO - (CONV_WIDTH - 1), tt), :] * cw_ref[j:j + 1, :]
    mu = jnp.mean(acc, axis=-1, keepdims=True)
    xc = acc - mu
    y = xc * lax.rsqrt(jnp.mean(xc * xc, axis=-1, keepdims=True) + EPS)
    y = y * g_ref[...] + b_ref[...]
    o_ref[0] = y * jax.nn.sigmoid(y)


def _conv_module(glu, prefix, conv_w, conv_b, ln_g, ln_b, tt):
    b, t, c = glu.shape
    assert tt >= HALO or t == tt, (t, tt)
    full = lambda a: pl.BlockSpec(a.shape, lambda bi, i: (0,) * a.ndim)
    cw = jnp.zeros((HALO, c), F32).at[:CONV_WIDTH].set(conv_w)
    return pl.pallas_call(
        functools.partial(_conv_kernel, tt=tt),
        grid=(b, t // tt),
        in_specs=[pl.BlockSpec((1, tt, c), lambda bi, i: (bi, i, 0)),
                  pl.BlockSpec((1, tt, c), lambda bi, i: (bi, jnp.maximum(i - 1, 0), 0)),
                  pl.BlockSpec((1, HALO, c), lambda bi, i: (bi, 0, 0)),
                  full(cw), pl.BlockSpec((1, c), lambda bi, i: (0, 0)),
                  pl.BlockSpec((1, c), lambda bi, i: (0, 0)), pl.BlockSpec((1, c), lambda bi, i: (0, 0))],
        out_specs=pl.BlockSpec((1, tt, c), lambda bi, i: (bi, i, 0)),
        out_shape=jax.ShapeDtypeStruct((b, t, c), F32),
        scratch_shapes=[pltpu.VMEM((HALO + tt, c), F32)],
        compiler_params=_cparams(("parallel", "arbitrary")),
        name="conv_module",
    )(glu, glu, prefix, cw, conv_b.reshape(1, c), ln_g.reshape(1, c), ln_b.reshape(1, c))


def _outproj_kernel(att_ref, conv_ref, x_ref, g1_ref, sh_ref, sc_ref, ga_ref, gc_ref, woa_ref, woc_ref,
                    g2_ref, wrh_ref, wrl_ref, br_ref, x1_ref, h2_ref, gates_ref):
    a = (_rms(att_ref[...]) * ga_ref[...]).astype(BF16)
    c = (_rms(conv_ref[...]) * gc_ref[...]).astype(BF16)
    mixed = _dot(a, woa_ref[...]) + _dot(c, woc_ref[...])
    x1 = x_ref[...] + g1_ref[0] * mixed
    x1_ref[...] = x1
    h2 = (_rms(x1) * g2_ref[...]) * (1.0 + sc_ref[0]) + sh_ref[0]
    hi = h2.astype(BF16)
    h2_ref[...] = hi
    lo = (h2 - hi.astype(F32)).astype(BF16)
    logits = _dot(hi, wrh_ref[...]) + _dot(lo, wrh_ref[...]) + _dot(hi, wrl_ref[...]) + br_ref[...]
    lane = lax.broadcasted_iota(I32, logits.shape, 1).astype(F32)
    neg_inf = -jnp.inf
    far = float(4 * LANES)
    is_g = jnp.logical_and(lane >= N_EXPERTS, lane < N_EXPERTS + N_GROUPS)
    gl = jnp.where(is_g, logits, neg_inf)
    gmax = jnp.max(gl, axis=-1, keepdims=True)
    gsel = jnp.min(jnp.where(gl == gmax, lane, far), axis=-1, keepdims=True) - N_EXPERTS
    p_g = 1.0 / jnp.sum(jnp.exp(gl - gmax), axis=-1, keepdims=True)
    in_grp = jnp.logical_and(lane >= gsel * EXPERTS_PER_GROUP, lane < (gsel + 1.0) * EXPERTS_PER_GROUP)
    el = jnp.where(in_grp, logits, neg_inf)
    v1 = jnp.max(el, axis=-1, keepdims=True)
    i1 = jnp.min(jnp.where(el == v1, lane, far), axis=-1, keepdims=True)
    el2 = jnp.where(lane == i1, neg_inf, el)
    v2 = jnp.max(el2, axis=-1, keepdims=True)
    i2 = jnp.min(jnp.where(el2 == v2, lane, far), axis=-1, keepdims=True)
    e2 = jnp.exp(v2 - v1)
    den = 1.0 + e2
    gates_ref[...] = (jnp.where(lane == i1, (1.0 / den) * p_g, 0.0)
                      + jnp.where(lane == i2, (e2 / den) * p_g, 0.0))


def _outproj(att, conv, x, gate1, shift2, scale2, ga, gc, woa, woc, g2, wrh, wrl, br, tm, tiles_per_group):
    n, d = x.shape
    rows = gate1.shape[1]
    full = lambda a: pl.BlockSpec(a.shape, lambda i: (0,) * a.ndim)
    row = lambda w: pl.BlockSpec((tm, w), lambda i: (i, 0))
    ms = _mod_spec(rows, d, tiles_per_group)
    return pl.pallas_call(
        _outproj_kernel,
        grid=(n // tm,),
        in_specs=[row(att.shape[1]), row(conv.shape[1]), row(d), ms, ms, ms,
                  full(ga), full(gc), full(woa), full(woc), full(g2), full(wrh), full(wrl), full(br)],
        out_specs=[row(d), row(d), row(LANES)],
        out_shape=[jax.ShapeDtypeStruct((n, d), F32), jax.ShapeDtypeStruct((n, d), BF16),
                   jax.ShapeDtypeStruct((n, LANES), F32)],
        compiler_params=_cparams(("parallel",)),
        name="outproj_router",
    )(att, conv, x, gate1, shift2, scale2, ga, gc, woa, woc, g2, wrh, wrl, br)


def _moe_kernel(h_ref, gates_ref, w1_ref, w3_ref, w2_ref, x1_ref, g2_ref, gf_ref, y_ref, acc_ref, *, f):
    j = pl.program_id(1)
    n_e = w1_ref.shape[1] // f

    @pl.when(j == 0)
    def _():
        acc_ref[...] = jnp.zeros(acc_ref.shape, F32)

    gates = gates_ref[...]
    lane = lax.broadcasted_iota(I32, gates.shape, 1)
    hb = h_ref[...]
    a = _dot(hb, w1_ref[...])
    b = _dot(hb, w3_ref[...])
    hid = (a * jax.nn.sigmoid(a)) * b
    gcols = [jnp.sum(jnp.where(lane == j * n_e + q, gates, 0.0), axis=-1, keepdims=True) for q in range(n_e)]
    hid = jnp.concatenate([hid[:, q * f:(q + 1) * f] * gcols[q] for q in range(n_e)], axis=1)
    acc_ref[...] += _dot(hid.astype(BF16), w2_ref[...])

    @pl.when(j == pl.num_programs(1) - 1)
    def _():
        xf = x1_ref[...] + g2_ref[0] * acc_ref[...]
        y_ref[...] = _rms(xf) * gf_ref[...]


def _moe(h2, gates, w1c, w3c, w2c, x1, gate2, g_final, tm, tiles_per_group, f):
    n, d = x1.shape
    wide = MOE_EXPERTS_PER_STEP * f
    rows = gate2.shape[1]
    row = lambda w: pl.BlockSpec((tm, w), lambda i, j: (i, 0))
    return pl.pallas_call(
        functools.partial(_moe_kernel, f=f),
        grid=(n // tm, w1c.shape[1] // wide),
        in_specs=[row(d), row(LANES),
                  pl.BlockSpec((d, wide), lambda i, j: (0, j)),
                  pl.BlockSpec((d, wide), lambda i, j: (0, j)),
                  pl.BlockSpec((wide, d), lambda i, j: (j, 0)),
                  row(d),
                  pl.BlockSpec((1, rows, d), lambda i, j: (i // tiles_per_group, 0, 0)),
                  pl.BlockSpec((1, d), lambda i, j: (0, 0))],
        out_specs=row(d),
        out_shape=jax.ShapeDtypeStruct((n, d), F32),
        scratch_shapes=[pltpu.VMEM((tm, d), F32)],
        compiler_params=_cparams(("parallel", "arbitrary")),
        name="moe_final",
    )(h2, gates, w1c, w3c, w2c, x1, gate2, g_final)


def _tile(n, pref):
    t = min(n, pref)
    assert n % t == 0, (n, pref)
    return t


def _prep_weights(w_in, w_out, w_rg, b_rg, w_re, b_re, w1, w3, w2):
    d = w_in.shape[0]
    off_qi = 3 * ATT_WIDTH
    off_ki = off_qi + IDX_HEADS * IDX_DIM
    off_conv = off_ki + IDX_DIM + IDX_HEADS
    cc = (w_in.shape[1] - off_conv) // 2
    wb = w_in.astype(BF16)
    wkw = jnp.zeros((d, LANES), BF16).at[:, :IDX_DIM + IDX_HEADS].set(wb[:, off_ki:off_conv])
    in_w = (wb[:, :off_qi], wb[:, off_qi:off_ki], wkw, wb[:, off_conv:off_conv + cc], wb[:, off_conv + cc:])
    wob = w_out.astype(BF16)
    out_w = (wob[:ATT_WIDTH], wob[ATT_WIDTH:])
    wr = jnp.zeros((d, LANES), F32)
    wr = wr.at[:, :N_EXPERTS].set(jnp.transpose(w_re, (1, 0, 2)).reshape(d, N_EXPERTS))
    wr = wr.at[:, N_EXPERTS:N_EXPERTS + N_GROUPS].set(w_rg)
    br = jnp.zeros((1, LANES), F32).at[0, :N_EXPERTS].set(b_re.reshape(-1))
    br = br.at[0, N_EXPERTS:N_EXPERTS + N_GROUPS].set(b_rg)
    wrh = wr.astype(BF16)
    wrl = (wr - wrh.astype(F32)).astype(BF16)
    n_e, _, f = w1.shape
    side_by_side = lambda w: jnp.transpose(w.astype(BF16), (1, 0, 2)).reshape(d, n_e * f)
    expert_w = (side_by_side(w1), side_by_side(w3), w2.astype(BF16).reshape(n_e * f, d), f)
    return in_w, out_w, (wrh, wrl, br), expert_w


def _token_layers(x2, mod_rows, tm, tm_moe, attn_fn, conv_fn, norms, weights):
    g1, ga, gc, g2, gf = norms
    in_w, out_w, router_w, expert_w = weights
    (shift1, tpg), (scale1, _), (gate1, _), (shift2, _), (scale2, _) = [mod_rows(j, tm) for j in range(5)]
    gate2, tpg_moe = mod_rows(5, tm_moe)
    q, k, v, kb, vb, qi, kw, kib, glu = _inproj(x2, shift1, scale1, g1, in_w, tm, tpg)
    att = attn_fn(q, k, v, kb, vb, qi, kw, kib)
    conv = conv_fn(glu)
    x1, h2, gates = _outproj(att, conv, x2, gate1, shift2, scale2, ga, gc, out_w[0], out_w[1], g2,
                             *router_w, tm, tpg)
    y = _moe(h2, gates, *expert_w[:3], x1, gate2, gf, tm_moe, tpg_moe, expert_w[3])
    return y, k, v, kw, glu


def kernel(x_prompt, x_sample, c_prompt, c_sample, cache_k, cache_v, cache_kidx, state_conv, page_table,
           w_ada, b_ada, g_norm1, w_in, conv_w, conv_b, conv_ln_g, conv_ln_b, g_attn_out, g_conv_out,
           w_out, g_norm2, w_rg, b_rg, w_re, b_re, w1, w3, w2, g_final):
    bp, s, d = x_prompt.shape
    bs, t_new, _ = x_sample.shape
    cc = conv_w.shape[1]
    n_pool, page = cache_k.shape[:2]
    hist = CONV_WIDTH - 1

    weights = _prep_weights(w_in, w_out, w_rg, b_rg, w_re, b_re, w1, w3, w2)
    norms = (g_norm1.reshape(1, d), g_attn_out.reshape(1, ATT_WIDTH), g_conv_out.reshape(1, cc),
             g_norm2.reshape(1, d), g_final.reshape(1, d))

    n_c = bp + bs
    pad = (-n_c) % 8
    c_all = jnp.concatenate([c_prompt, c_sample, jnp.zeros((pad, d), F32)], axis=0)
    mod = _modulation(c_all, w_ada, b_ada).reshape(n_c + pad, 6, d)

    tm_p = _tile(s, 256)

    def mod_p(j, tile):
        return mod[:bp, j].reshape(bp, 1, d), s // tile

    def prompt_attn(q, k, v, kb, vb, qi, kw, kib):
        r3 = lambda a: a.reshape(bp, s, a.shape[-1])
        tq = _tile(s, 512)
        tk = _tile(s, 512)
        return _prompt_attention(r3(q), r3(qi), r3(kw), r3(kb), r3(vb), r3(kib), tq, tk).reshape(bp * s, ATT_WIDTH)

    def prompt_conv(glu):
        prefix = jnp.zeros((bp, HALO, cc), F32)
        tt = _tile(s, 512)
        return _conv_module(glu.reshape(bp, s, cc), prefix, conv_w, conv_b, conv_ln_g, conv_ln_b, tt).reshape(bp * s, cc)

    y_p, k_p, v_p, kw_p, glu_p = _token_layers(
        x_prompt.reshape(bp * s, d), mod_p, tm_p, _tile(s, 512), prompt_attn, prompt_conv, norms, weights)

    kidx_t = jnp.transpose(cache_kidx, (0, 2, 1))
    k_t = jnp.transpose(cache_k, (0, 2, 3, 1)).reshape(n_pool, ATT_WIDTH, page)
    v_t = jnp.transpose(cache_v, (0, 2, 3, 1)).reshape(n_pool, ATT_WIDTH, page)
    n_s = bs * t_new
    tm_s = _tile(n_s, 256)
    mod_tok = jnp.repeat(mod[bp:bp + bs], t_new, axis=0)

    def mod_s(j, tile):
        return mod_tok[:, j].reshape(n_s // tile, tile, d), 1

    def sample_attn(q, k, v, kb, vb, qi, kw, kib):
        tp = 8 - t_new
        qi_rows = jnp.pad(qi.reshape(bs, t_new, IDX_HEADS, IDX_DIM).transpose(0, 2, 1, 3),
                          ((0, 0), (0, 0), (0, tp), (0, 0))).reshape(bs, 64, IDX_DIM)
        wi = kw[:, IDX_DIM:IDX_DIM + IDX_HEADS].reshape(bs, t_new, IDX_HEADS).transpose(0, 2, 1)
        w_rows = jnp.broadcast_to(jnp.pad(wi, ((0, 0), (0, 0), (0, tp))).reshape(bs, 64, 1), (bs, 64, LANES))
        rows_pad = lambda a: jnp.pad(a.reshape(bs, t_new, a.shape[-1]), ((0, 0), (0, LANES - t_new), (0, 0)))
        scores = _sample_scores(page_table, kidx_t, qi_rows, w_rows, rows_pad(kib), t_new)
        topk = min(TOPK_MAX, (page_table.shape[1] * page + t_new) // 4)
        bias = _sample_select(scores, topk, _tile(bs, SELECT_GROUP))
        qh = jnp.pad(q.reshape(bs, t_new, N_HEADS, HEAD_DIM).transpose(0, 2, 1, 3),
                     ((0, 0), (0, 0), (0, tp), (0, 0)))
        eye = jnp.eye(N_HEADS, dtype=q.dtype)
        q_bd = (qh[:, :, :, None, :] * eye[None, :, None, :, None]).reshape(bs, 64, ATT_WIDTH)
        o = _sample_attention(page_table, k_t, v_t, q_bd, bias, rows_pad(kb), rows_pad(vb))
        return o[:, :t_new].reshape(n_s, ATT_WIDTH)

    def sample_conv(glu):
        prefix = jnp.pad(state_conv, ((0, 0), (HALO - hist, 0), (0, 0)))
        glu8 = jnp.pad(glu.reshape(bs, t_new, cc), ((0, 0), (0, 8 - t_new), (0, 0)))
        return _conv_module(glu8, prefix, conv_w, conv_b, conv_ln_g, conv_ln_b, 8)[:, :t_new].reshape(n_s, cc)

    y_s, k_s, v_s, kw_s, glu_s = _token_layers(
        x_sample.reshape(n_s, d), mod_s, tm_s, _tile(n_s, 512), sample_attn, sample_conv, norms, weights)

    glu_p3 = glu_p.reshape(bp, s, cc)
    conv_prompt = glu_p3[:, s - hist:]
    conv_sample = jnp.concatenate([state_conv, glu_s.reshape(bs, t_new, cc)], axis=1)[:, -hist:]
    return (y_p.reshape(bp, s, d), y_s.reshape(bs, t_new, d),
            k_p.reshape(bp, s, N_HEADS, HEAD_DIM), v_p.reshape(bp, s, N_HEADS, HEAD_DIM),
            kw_p[:, :IDX_DIM].reshape(bp, s, IDX_DIM), conv_prompt,
            k_s.reshape(bs, t_new, N_HEADS, HEAD_DIM), v_s.reshape(bs, t_new, N_HEADS, HEAD_DIM),
            kw_s[:, :IDX_DIM].reshape(bs, t_new, IDX_DIM), conv_sample)
```

```python
import functools

import jax
import jax.numpy as jnp
from jax import lax
from jax.experimental import pallas as pl
from jax.experimental.pallas import tpu as pltpu

F32 = jnp.float32
BF16 = jnp.bfloat16
I32 = jnp.int32

EPS = 1e-6
HEAD_DIM = 64
N_HEADS = 8
ATT_WIDTH = N_HEADS * HEAD_DIM
IDX_HEADS = 8
IDX_DIM = 64
CONV_WIDTH = 31
TOPK_MAX = 256
N_GROUPS = 4
EXPERTS_PER_GROUP = 8
N_EXPERTS = N_GROUPS * EXPERTS_PER_GROUP
LANES = 128
HALO = 32
PAGES_PER_STEP = 32
SELECT_GROUP = 16
SELECT_ROWS = 128
PAIRS_PER_LOOP = 2
MOE_EXPERTS_PER_STEP = 4
NEG_BIG = -1e30
MAX_BISECT = 24
KEEP_ALL = 1e9
VMEM_LIMIT = 56 * 1024 * 1024


def _cparams(sem):
    return pltpu.CompilerParams(dimension_semantics=sem, vmem_limit_bytes=VMEM_LIMIT)


def _rms(x):
    return x * lax.rsqrt(jnp.mean(x * x, axis=-1, keepdims=True) + EPS)


def _dot(a, b):
    return jnp.dot(a, b, preferred_element_type=F32)


def _dot_nt(a, b):
    return lax.dot_general(a, b, (((1,), (1,)), ((), ())), preferred_element_type=F32)


def _mod_kernel(c_ref, w_ref, b_ref, o_ref):
    c = c_ref[...]
    a = (c * jax.nn.sigmoid(c)).astype(BF16)
    o_ref[...] = _dot(a, w_ref[...].astype(BF16)) + b_ref[...]


def _modulation(c, w_ada, b_ada, tn=512):
    r, d = c.shape
    n = w_ada.shape[1]
    return pl.pallas_call(
        _mod_kernel,
        grid=(n // tn,),
        in_specs=[pl.BlockSpec((r, d), lambda j: (0, 0)),
                  pl.BlockSpec((d, tn), lambda j: (0, j)),
                  pl.BlockSpec((1, tn), lambda j: (0, j))],
        out_specs=pl.BlockSpec((r, tn), lambda j: (0, j)),
        out_shape=jax.ShapeDtypeStruct((r, n), F32),
        compiler_params=_cparams(("arbitrary",)),
        name="adaln_mod",
    )(c, w_ada, b_ada.reshape(1, n))


def _inproj_kernel(x_ref, sh_ref, sc_ref, g_ref, wqkv_ref, wqi_ref, wkw_ref, wca_ref, wcb_ref,
                   q_ref, k_ref, v_ref, kb_ref, vb_ref, qi_ref, kw_ref, kib_ref, glu_ref, *, kv_seq_minor):
    h = _rms(x_ref[...]) * g_ref[...]
    h = h * (1.0 + sc_ref[0]) + sh_ref[0]
    hb = h.astype(BF16)
    qkv = _dot(hb, wqkv_ref[...])
    q_ref[...] = (qkv[:, :ATT_WIDTH] * (HEAD_DIM ** -0.5)).astype(BF16)
    k = qkv[:, ATT_WIDTH:2 * ATT_WIDTH]
    v = qkv[:, 2 * ATT_WIDTH:]
    if kv_seq_minor:
        k_ref[0] = k.T
        v_ref[0] = v.T
    else:
        k_ref[...] = k
        v_ref[...] = v
    kb_ref[...] = k.astype(BF16)
    vb_ref[...] = v.astype(BF16)
    qi_ref[...] = _dot(hb, wqi_ref[...]).astype(BF16)
    kw = _dot(hb, wkw_ref[...])
    kw_ref[...] = kw
    kib_ref[...] = kw[:, :IDX_DIM].astype(BF16)
    a = _dot(hb, wca_ref[...])
    b = _dot(hb, wcb_ref[...])
    glu_ref[...] = a * jax.nn.sigmoid(b)


def _mod_spec(rows, d, tiles_per_group):
    return pl.BlockSpec((1, rows, d), lambda i: (i // tiles_per_group, 0, 0))


def _inproj(x, shift, scale, g1, wts, tm, tiles_per_group, seq_len=None):
    n, d = x.shape
    wqkv, wqi, wkw, wca, wcb = wts
    cc = wca.shape[1]
    rows = shift.shape[1]
    full = lambda a: pl.BlockSpec(a.shape, lambda i: (0,) * a.ndim)
    row = lambda w: pl.BlockSpec((tm, w), lambda i: (i, 0))
    outs = [
        (ATT_WIDTH, BF16),
        (ATT_WIDTH, F32),
        (ATT_WIDTH, F32),
        (ATT_WIDTH, BF16),
        (ATT_WIDTH, BF16),
        (IDX_HEADS * IDX_DIM, BF16),
        (LANES, F32),
        (IDX_DIM, BF16),
        (cc, F32),
    ]
    out_specs = [row(w) for w, _ in outs]
    out_shape = [jax.ShapeDtypeStruct((n, w), dt) for w, dt in outs]
    if seq_len is not None:
        per_seq = seq_len // tm
        for o in (1, 2):
            out_specs[o] = pl.BlockSpec((1, ATT_WIDTH, tm), lambda i: (i // per_seq, 0, i % per_seq))
            out_shape[o] = jax.ShapeDtypeStruct((n // seq_len, ATT_WIDTH, seq_len), F32)
    return pl.pallas_call(
        functools.partial(_inproj_kernel, kv_seq_minor=seq_len is not None),
        grid=(n // tm,),
        in_specs=[row(d), _mod_spec(rows, d, tiles_per_group), _mod_spec(rows, d, tiles_per_group),
                  full(g1), full(wqkv), full(wqi), full(wkw), full(wca), full(wcb)],
        out_specs=out_specs,
        out_shape=out_shape,
        compiler_params=_cparams(("parallel",)),
        name="inproj",
    )(x, shift, scale, g1, wqkv, wqi, wkw, wca, wcb)


def _fold(scan, rows, init, fn, red):
    def body(sc, part, sl=slice(None)):
        for g in range(sc.shape[-1] // LANES):
            part = fn(part, sc[..., g * LANES:(g + 1) * LANES], sl)
        return part
    return red(scan(body, jnp.full(rows + (LANES,), init, F32)), axis=-1, keepdims=True)


def _kth_select(scan, rows, k):
    inf = jnp.inf
    count_ge = lambda t: _fold(scan, rows, 0.0, lambda p, x, sl: p + jnp.where(x >= t[sl], 1.0, 0.0), jnp.sum)
    count_gt = lambda t: _fold(scan, rows, 0.0, lambda p, x, sl: p + jnp.where(x > t[sl], 1.0, 0.0), jnp.sum)
    min_ge = lambda t: _fold(scan, rows, inf,
                             lambda p, x, sl: jnp.minimum(p, jnp.where(x >= t[sl], x, inf)), jnp.min)
    min_gt = lambda t: _fold(scan, rows, inf,
                             lambda p, x, sl: jnp.minimum(p, jnp.where(x > t[sl], x, inf)), jnp.min)
    row_max = _fold(scan, rows, -inf, lambda p, x, sl: jnp.maximum(p, x), jnp.max)
    row_min = min_gt(jnp.full(rows + (1,), -inf, F32))
    n_valid = count_ge(row_min)
    any_row = lambda flag: jnp.max(flag) > 0.5

    def bisect(lo, hi, cnt, pending):
        def cond(st):
            it, _, _, _, act = st
            return jnp.logical_and(it < MAX_BISECT, any_row(act))

        def body(st):
            it, lo, hi, cnt, act = st
            mid = 0.5 * (lo + hi)
            c = count_ge(mid)
            go_up = jnp.where(c >= k, act, 0.0) > 0.5
            go_dn = jnp.where(c >= k, 0.0, act) > 0.5
            lo2 = jnp.where(go_up, mid, lo)
            cnt2 = jnp.where(go_up, c, cnt)
            hi2 = jnp.where(go_dn, mid, hi)
            moving = jnp.logical_and(jnp.logical_and(mid > lo, mid < hi), cnt2 > k)
            return it + 1, lo2, hi2, cnt2, jnp.where(moving, act, 0.0)

        act0 = jnp.where(cnt > k, pending, 0.0)
        _, lo, hi, cnt, _ = lax.while_loop(cond, body, (jnp.int32(0), lo, hi, cnt, act0))
        return lo, hi, cnt

    def outer_cond(st):
        return any_row(st[5])

    def outer_body(st):
        lo, hi, cnt, thr, need, pending = st
        lo, hi, cnt = bisect(lo, hi, cnt, pending)
        plain = jnp.where(cnt <= k, pending, 0.0)
        hard = jnp.where(cnt <= k, 0.0, pending)
        thr = jnp.where(plain > 0.5, lo, thr)

        def resolve(_):
            t = min_ge(lo)
            n_gt = count_gt(t)
            tied = jnp.where(n_gt < k, hard, 0.0) > 0.5
            above_f = jnp.where(n_gt < k, 0.0, hard)
            above = above_f > 0.5
            return (jnp.where(above, min_gt(t), lo), jnp.where(above, n_gt, cnt),
                    jnp.where(tied, t, thr), jnp.where(tied, k - n_gt, need), above_f)

        def settled(_):
            return lo, cnt, thr, need, jnp.zeros(rows + (1,), F32)

        lo, cnt, thr, need, pending = lax.cond(any_row(hard), resolve, settled, 0)
        return lo, hi, cnt, thr, need, pending

    lo0 = row_min
    hi0 = row_max + (row_max - row_min) + 1.0
    st0 = (lo0, hi0, n_valid, lo0, jnp.full(rows + (1,), KEEP_ALL, F32), jnp.ones(rows + (1,), F32))
    _, _, _, thr, need, _ = lax.while_loop(outer_cond, outer_body, st0)
    return thr, need


def _prompt_attn_kernel(q_ref, qi_ref, kw_ref, kb_ref, vb_ref, kib_ref, o_ref,
                        score_ref, bias_ref, *, tq, tk, topk):
    i = pl.program_id(1)
    nk = ((i + 1) * tq + tk - 1) // tk
    lane = lax.broadcasted_iota(I32, (tq, tk), 1)
    qpos = i * tq + lax.broadcasted_iota(I32, (tq, tk), 0)
    wi = kw_ref[0][:, IDX_DIM:IDX_DIM + IDX_HEADS]
    qi = qi_ref[0]

    def score_chunk(c, carry):
        kic = kib_ref[0, pl.ds(pl.multiple_of(c * tk, tk), tk), :]
        acc = jnp.zeros((tq, tk), F32)
        for h in range(IDX_HEADS):
            s = _dot_nt(qi[:, h * IDX_DIM:(h + 1) * IDX_DIM], kic)
            acc = acc + wi[:, h:h + 1] * jnp.maximum(s, 0.0)
        valid = (c * tk + lane) <= qpos
        score_ref[c] = jnp.where(valid, acc, -jnp.inf)
        return carry
    lax.fori_loop(0, nk, score_chunk, 0)

    def scan(body, init):
        outs = []
        for r in range(0, tq, SELECT_ROWS):
            sl = slice(r, min(r + SELECT_ROWS, tq))
            outs.append(lax.fori_loop(0, nk, lambda c, part, sl=sl: body(score_ref[c, sl, :], part, sl),
                                      init[sl]))
        return jnp.concatenate(outs, axis=0)
    thr, need = _kth_select(scan, (tq,), topk)

    def bias_chunk(c, carry):
        bias_ref[c] = jnp.where(score_ref[c] >= thr, 0.0, NEG_BIG)
        return carry
    lax.fori_loop(0, nk, bias_chunk, 0)

    tie_f = jnp.where(need < KEEP_ALL, 1.0, 0.0)

    @pl.when(jnp.max(tie_f) > 0.5)
    def _():
        upper = (lax.broadcasted_iota(I32, (tk, tk), 0)
                 < lax.broadcasted_iota(I32, (tk, tk), 1)).astype(BF16)

        def tie_chunk(c, before):
            sc = score_ref[c]
            eq = sc == thr
            eqf = jnp.where(eq, 1.0, 0.0)
            rank = before + _dot(eqf.astype(BF16), upper)
            tie_bias = jnp.where(jnp.logical_or(sc > thr, jnp.logical_and(eq, rank < need)), 0.0, NEG_BIG)
            bias_ref[c] = jnp.where(tie_f > 0.5, tie_bias, bias_ref[c])
            return before + jnp.sum(eqf, axis=-1, keepdims=True)
        lax.fori_loop(0, nk, tie_chunk, jnp.zeros((tq, 1), F32))

    half = lax.broadcasted_iota(I32, (tq, LANES), 1) < HEAD_DIM
    for p0 in range(0, N_HEADS // 2, PAIRS_PER_LOOP):
        slabs = [slice(p * LANES, (p + 1) * LANES) for p in range(p0, p0 + PAIRS_PER_LOOP)]
        qs = []
        for sl in slabs:
            qp = q_ref[0, :, sl]
            qs.append((jnp.where(half, qp, jnp.zeros_like(qp)), jnp.where(half, jnp.zeros_like(qp), qp)))

        def attn_chunk(c, carry):
            off = pl.multiple_of(c * tk, tk)
            bias = bias_ref[c]
            new = []
            for n, sl in enumerate(slabs):
                kc = kb_ref[0, pl.ds(off, tk), sl]
                vc = vb_ref[0, pl.ds(off, tk), sl]
                for e in range(2):
                    m, l, acc = carry[2 * n + e]
                    s = _dot_nt(qs[n][e], kc) + bias
                    m_new = jnp.maximum(m, jnp.max(s, axis=-1, keepdims=True))
                    alpha = jnp.exp(m - m_new)
                    pr = jnp.exp(s - m_new)
                    l = alpha * l + jnp.sum(pr, axis=-1, keepdims=True)
                    acc = alpha * acc + _dot(pr.astype(BF16), vc)
                    new.append((m_new, l, acc))
            return tuple(new)

        init = tuple((jnp.full((tq, 1), NEG_BIG, F32), jnp.zeros((tq, 1), F32),
                      jnp.zeros((tq, LANES), F32)) for _ in range(2 * PAIRS_PER_LOOP))
        res = lax.fori_loop(0, nk, attn_chunk, init)
        for n, sl in enumerate(slabs):
            (_, l0, a0), (_, l1, a1) = res[2 * n], res[2 * n + 1]
            o_ref[0, :, sl] = jnp.where(half, a0 / l0, a1 / l1)


def _prompt_attention(q, qi, kw, kb, vb, kib, tq, tk):
    b, s, _ = q.shape
    topk = min(TOPK_MAX, s // 4)
    blk = lambda w: pl.BlockSpec((1, tq, w), lambda bi, i: (bi, i, 0))
    seq = lambda w: pl.BlockSpec((1, s, w), lambda bi, i: (bi, 0, 0))
    return pl.pallas_call(
        functools.partial(_prompt_attn_kernel, tq=tq, tk=tk, topk=topk),
        grid=(b, s // tq),
        in_specs=[blk(ATT_WIDTH), blk(IDX_HEADS * IDX_DIM), blk(LANES),
                  seq(ATT_WIDTH), seq(ATT_WIDTH), seq(IDX_DIM)],
        out_specs=blk(ATT_WIDTH),
        out_shape=jax.ShapeDtypeStruct((b, s, ATT_WIDTH), F32),
        scratch_shapes=[pltpu.VMEM((s // tk, tq, tk), F32), pltpu.VMEM((s // tk, tq, tk), F32)],
        compiler_params=_cparams(("parallel", "arbitrary")),
        name="prompt_dsa_attention",
    )(q, qi, kw, kb, vb, kib)


def _sample_scores_kernel(pt_ref, *refs, n_pages, t_new):
    del pt_ref
    pages = refs[:n_pages]
    qi_ref, w_ref, kin_ref, score_ref = refs[n_pages:]
    qi = qi_ref[0]
    w = w_ref[0]

    def combine(s):
        acc = jnp.zeros((8, LANES), F32)
        for h in range(IDX_HEADS):
            acc = acc + w[h * 8:(h + 1) * 8] * jnp.maximum(s[h * 8:(h + 1) * 8], 0.0)
        return acc

    for j in range(n_pages):
        page_t = pages[j][...].astype(BF16)
        score_ref[j] = combine(_dot(qi, page_t))

    s_new = combine(_dot_nt(qi, kin_ref[0]))
    lane = lax.broadcasted_iota(I32, (8, LANES), 1)
    row = lax.broadcasted_iota(I32, (8, LANES), 0)
    valid = jnp.logical_and(lane < t_new, lane <= row)
    score_ref[n_pages] = jnp.where(valid, s_new, -jnp.inf)


def _sample_scores(page_table, kidx_t, qi_rows, w_rows, ki_new, t_new):
    b, n_pages = page_table.shape
    page = kidx_t.shape[2]

    def page_spec(j):
        return pl.BlockSpec((None, IDX_DIM, page), lambda bi, pt: (pt[bi, j], 0, 0))
    per_seq = lambda shp: pl.BlockSpec((1,) + shp, lambda bi, pt: (bi,) + (0,) * len(shp))
    grid_spec = pltpu.PrefetchScalarGridSpec(
        num_scalar_prefetch=1,
        grid=(b,),
        in_specs=[page_spec(j) for j in range(n_pages)]
                 + [per_seq((64, IDX_DIM)), per_seq((64, LANES)), per_seq((LANES, IDX_DIM))],
        out_specs=pl.BlockSpec((n_pages + 1, None, 8, LANES), lambda bi, pt: (0, bi, 0, 0)),
    )
    return pl.pallas_call(
        functools.partial(_sample_scores_kernel, n_pages=n_pages, t_new=t_new),
        grid_spec=grid_spec,
        out_shape=jax.ShapeDtypeStruct((n_pages + 1, b, 8, LANES), F32),
        compiler_params=_cparams(("parallel",)),
        name="sample_dsa_scores",
    )(page_table, *([kidx_t] * n_pages), qi_rows, w_rows, ki_new)


def _sample_select_kernel(score_ref, bias_ref, *, topk):
    n_chunks, g = score_ref.shape[:2]
    unroll = max(u for u in range(1, 9) if n_chunks % u == 0)

    def scan(body, init):
        def step(c, part):
            for u in range(unroll):
                part = body(score_ref[c * unroll + u], part)
            return part
        return lax.fori_loop(0, n_chunks // unroll, step, init)
    thr, need = _kth_select(scan, (g, 8), topk)

    def bias_page(c, carry):
        bias_ref[c] = jnp.where(score_ref[c] >= thr, 0.0, NEG_BIG)
        return carry
    lax.fori_loop(0, n_chunks, bias_page, 0)

    tie_f = jnp.where(need < KEEP_ALL, 1.0, 0.0)

    @pl.when(jnp.max(tie_f) > 0.5)
    def _():
        upper = (lax.broadcasted_iota(I32, (LANES, LANES), 0)
                 < lax.broadcasted_iota(I32, (LANES, LANES), 1)).astype(BF16)

        def tie_page(c, before):
            sc = score_ref[c]
            eq = sc == thr
            eqf = jnp.where(eq, 1.0, 0.0)
            low = _dot(eqf.reshape(g * 8, LANES).astype(BF16), upper).reshape(g, 8, LANES)
            rank = before + low
            tie_bias = jnp.where(jnp.logical_or(sc > thr, jnp.logical_and(eq, rank < need)), 0.0, NEG_BIG)
            bias_ref[c] = jnp.where(tie_f > 0.5, tie_bias, bias_ref[c])
            return before + jnp.sum(eqf, axis=-1, keepdims=True)
        lax.fori_loop(0, n_chunks, tie_page, jnp.zeros((g, 8, 1), F32))


def _sample_select(scores, topk, group):
    n_chunks, b = scores.shape[:2]
    spec = pl.BlockSpec((n_chunks, group, 8, LANES), lambda i: (0, i, 0, 0))
    return pl.pallas_call(
        functools.partial(_sample_select_kernel, topk=topk),
        grid=(b // group,),
        in_specs=[spec],
        out_specs=spec,
        out_shape=jax.ShapeDtypeStruct(scores.shape, F32),
        compiler_params=_cparams(("parallel",)),
        name="sample_dsa_select",
    )(scores)


def _sample_attn_kernel(pt_ref, *refs, n_steps):
    del pt_ref
    kpages = refs[:PAGES_PER_STEP]
    vpages = refs[PAGES_PER_STEP:2 * PAGES_PER_STEP]
    (q_ref, bias_ref, bias_new_ref, kn_ref, vn_ref, o_ref,
     kt_ref, vt_ref, m_ref, l_ref, acc_ref) = refs[2 * PAGES_PER_STEP:]
    g = pl.program_id(1)
    q = q_ref[0]

    @pl.when(g == 0)
    def _():
        m_ref[...] = jnp.full(m_ref.shape, NEG_BIG, F32)
        l_ref[...] = jnp.zeros(l_ref.shape, F32)
        acc_ref[...] = jnp.zeros(acc_ref.shape, F32)

    def update(s, pv):
        m = m_ref[...]
        m_new = jnp.maximum(m, jnp.max(s, axis=-1, keepdims=True))
        alpha = jnp.exp(m - m_new)
        pr = jnp.exp(s - m_new)
        l_ref[...] = alpha * l_ref[...] + jnp.sum(pr, axis=-1, keepdims=True)
        acc_ref[...] = alpha * acc_ref[...] + pv(pr.astype(BF16))
        m_ref[...] = m_new

    tile8 = lambda b8: jnp.concatenate([b8] * N_HEADS, axis=0)

    for j in range(PAGES_PER_STEP):
        kt_ref[:, j * LANES:(j + 1) * LANES] = kpages[j][...].astype(BF16)
        vt_ref[:, j * LANES:(j + 1) * LANES] = vpages[j][...].astype(BF16)
    s = _dot(q, kt_ref[...])
    s = jnp.concatenate([s[:, j * LANES:(j + 1) * LANES] + tile8(bias_ref[j])
                         for j in range(PAGES_PER_STEP)], axis=1)
    update(s, lambda pr: _dot_nt(pr, vt_ref[...]))

    @pl.when(g == n_steps - 1)
    def _():
        update(_dot_nt(q, kn_ref[0]) + tile8(bias_new_ref[0]), lambda pr: _dot(pr, vn_ref[0]))
        o = acc_ref[...] / l_ref[...]
        lane = lax.broadcasted_iota(I32, (8, ATT_WIDTH), 1)
        out = jnp.zeros((8, ATT_WIDTH), F32)
        for h in range(N_HEADS):
            in_head = jnp.logical_and(lane >= h * HEAD_DIM, lane < (h + 1) * HEAD_DIM)
            out = out + jnp.where(in_head, o[h * 8:(h + 1) * 8], 0.0)
        o_ref[0] = out


def _sample_attention(page_table, k_t, v_t, q_bd, bias, k_new, v_new):
    b, n_pages = page_table.shape
    page = k_t.shape[2]
    n_steps = n_pages // PAGES_PER_STEP

    def page_spec(j):
        return pl.BlockSpec((None, ATT_WIDTH, page),
                            lambda bi, g, pt: (pt[bi, g * PAGES_PER_STEP + j], 0, 0))
    per_seq = lambda shp: pl.BlockSpec((1,) + shp, lambda bi, g, pt: (bi,) + (0,) * len(shp))
    grid_spec = pltpu.PrefetchScalarGridSpec(
        num_scalar_prefetch=1,
        grid=(b, n_steps),
        in_specs=[page_spec(j) for j in range(PAGES_PER_STEP)] * 2
                 + [per_seq((64, ATT_WIDTH)),
                    pl.BlockSpec((PAGES_PER_STEP, None, 8, LANES), lambda bi, g, pt: (g, bi, 0, 0)),
                    pl.BlockSpec((1, None, 8, LANES), lambda bi, g, pt: (n_pages, bi, 0, 0)),
                    per_seq((LANES, ATT_WIDTH)), per_seq((LANES, ATT_WIDTH))],
        out_specs=per_seq((8, ATT_WIDTH)),
        scratch_shapes=[pltpu.VMEM((ATT_WIDTH, PAGES_PER_STEP * page), BF16),
                        pltpu.VMEM((ATT_WIDTH, PAGES_PER_STEP * page), BF16),
                        pltpu.VMEM((64, 1), F32), pltpu.VMEM((64, 1), F32),
                        pltpu.VMEM((64, ATT_WIDTH), F32)],
    )
    return pl.pallas_call(
        functools.partial(_sample_attn_kernel, n_steps=n_steps),
        grid_spec=grid_spec,
        out_shape=jax.ShapeDtypeStruct((b, 8, ATT_WIDTH), F32),
        compiler_params=_cparams(("parallel", "arbitrary")),
        name="sample_dsa_attention",
    )(page_table, *([k_t] * PAGES_PER_STEP), *([v_t] * PAGES_PER_STEP),
      q_bd, bias, bias, k_new, v_new)


def _conv_kernel(cur_ref, prev_ref, pre_ref, cw_ref, cb_ref, g_ref, b_ref, o_ref, win_ref, *, tt):
    i = pl.program_id(1)

    @pl.when(i == 0)
    def _():
        win_ref[0:HALO] = pre_ref[0]

    if tt >= HALO:
        @pl.when(i > 0)
        def _():
            win_ref[0:HALO] = prev_ref[0, tt - HALO:tt, :]

    win_ref[HALO:HALO + tt] = cur_ref[0]
    acc = jnp.zeros(o_ref.shape[1:], F32) + cb_ref[...]
    for j in range(CONV_WIDTH):
        acc = acc + win_ref[pl.ds(j + HALO - (CONV_WIDTH - 1), tt), :] * cw_ref[j:j + 1, :]
    mu = jnp.mean(acc, axis=-1, keepdims=True)
    xc = acc - mu
    y = xc * lax.rsqrt(jnp.mean(xc * xc, axis=-1, keepdims=True) + EPS)
    y = y * g_ref[...] + b_ref[...]
    o_ref[0] = y * jax.nn.sigmoid(y)


def _conv_module(glu, prefix, conv_w, conv_b, ln_g, ln_b, tt):
    b, t, c = glu.shape
    assert tt >= HALO or t == tt, (t, tt)
    full = lambda a: pl.BlockSpec(a.shape, lambda bi, i: (0,) * a.ndim)
    cw = jnp.zeros((HALO, c), F32).at[:CONV_WIDTH].set(conv_w)
    return pl.pallas_call(
        functools.partial(_conv_kernel, tt=tt),
        grid=(b, t // tt),
        in_specs=[pl.BlockSpec((1, tt, c), lambda bi, i: (bi, i, 0)),
                  pl.BlockSpec((1, tt, c), lambda bi, i: (bi, jnp.maximum(i - 1, 0), 0)),
                  pl.BlockSpec((1, HALO, c), lambda bi, i: (bi, 0, 0)),
                  full(cw), pl.BlockSpec((1, c), lambda bi, i: (0, 0)),
                  pl.BlockSpec((1, c), lambda bi, i: (0, 0)), pl.BlockSpec((1, c), lambda bi, i: (0, 0))],
        out_specs=pl.BlockSpec((1, tt, c), lambda bi, i: (bi, i, 0)),
        out_shape=jax.ShapeDtypeStruct((b, t, c), F32),
        scratch_shapes=[pltpu.VMEM((HALO + tt, c), F32)],
        compiler_params=_cparams(("parallel", "arbitrary")),
        name="conv_module",
    )(glu, glu, prefix, cw, conv_b.reshape(1, c), ln_g.reshape(1, c), ln_b.reshape(1, c))


def _conv_tail_kernel(buf_ref, cw_ref, cb_ref, g_ref, b_ref, o_ref):
    acc = jnp.zeros(o_ref.shape, F32) + cb_ref[...]
    for j in range(CONV_WIDTH):
        acc = acc + buf_ref[:, pl.ds(j + HALO - (CONV_WIDTH - 1), 8), :] * cw_ref[j:j + 1, :]
    mu = jnp.mean(acc, axis=-1, keepdims=True)
    xc = acc - mu
    y = xc * lax.rsqrt(jnp.mean(xc * xc, axis=-1, keepdims=True) + EPS)
    y = y * g_ref[...] + b_ref[...]
    o_ref[...] = y * jax.nn.sigmoid(y)


def _conv_module_tail(buf, conv_w, conv_b, ln_g, ln_b, group):
    b, rows, c = buf.shape
    cw = jnp.zeros((HALO, c), F32).at[:CONV_WIDTH].set(conv_w)
    vec = pl.BlockSpec((1, c), lambda i: (0, 0))
    return pl.pallas_call(
        _conv_tail_kernel,
        grid=(b // group,),
        in_specs=[pl.BlockSpec((group, rows, c), lambda i: (i, 0, 0)),
                  pl.BlockSpec((HALO, c), lambda i: (0, 0)), vec, vec, vec],
        out_specs=pl.BlockSpec((group, 8, c), lambda i: (i, 0, 0)),
        out_shape=jax.ShapeDtypeStruct((b, 8, c), F32),
        compiler_params=_cparams(("parallel",)),
        name="conv_module_tail",
    )(buf, cw, conv_b.reshape(1, c), ln_g.reshape(1, c), ln_b.reshape(1, c))


def _outproj_kernel(att_ref, conv_ref, x_ref, g1_ref, sh_ref, sc_ref, ga_ref, gc_ref, woa_ref, woc_ref,
                    g2_ref, wrh_ref, wrl_ref, br_ref, x1_ref, h2_ref, gates_ref):
    a = (_rms(att_ref[...]) * ga_ref[...]).astype(BF16)
    c = (_rms(conv_ref[...]) * gc_ref[...]).astype(BF16)
    mixed = _dot(a, woa_ref[...]) + _dot(c, woc_ref[...])
    x1 = x_ref[...] + g1_ref[0] * mixed
    x1_ref[...] = x1
    h2 = (_rms(x1) * g2_ref[...]) * (1.0 + sc_ref[0]) + sh_ref[0]
    hi = h2.astype(BF16)
    h2_ref[...] = hi
    lo = (h2 - hi.astype(F32)).astype(BF16)
    logits = _dot(hi, wrh_ref[...]) + _dot(lo, wrh_ref[...]) + _dot(hi, wrl_ref[...]) + br_ref[...]
    lane = lax.broadcasted_iota(I32, logits.shape, 1).astype(F32)
    neg_inf = -jnp.inf
    far = float(4 * LANES)
    is_g = jnp.logical_and(lane >= N_EXPERTS, lane < N_EXPERTS + N_GROUPS)
    gl = jnp.where(is_g, logits, neg_inf)
    gmax = jnp.max(gl, axis=-1, keepdims=True)
    gsel = jnp.min(jnp.where(gl == gmax, lane, far), axis=-1, keepdims=True) - N_EXPERTS
    p_g = 1.0 / jnp.sum(jnp.exp(gl - gmax), axis=-1, keepdims=True)
    in_grp = jnp.logical_and(lane >= gsel * EXPERTS_PER_GROUP, lane < (gsel + 1.0) * EXPERTS_PER_GROUP)
    el = jnp.where(in_grp, logits, neg_inf)
    v1 = jnp.max(el, axis=-1, keepdims=True)
    i1 = jnp.min(jnp.where(el == v1, lane, far), axis=-1, keepdims=True)
    el2 = jnp.where(lane == i1, neg_inf, el)
    v2 = jnp.max(el2, axis=-1, keepdims=True)
    i2 = jnp.min(jnp.where(el2 == v2, lane, far), axis=-1, keepdims=True)
    e2 = jnp.exp(v2 - v1)
    den = 1.0 + e2
    gates_ref[...] = (jnp.where(lane == i1, (1.0 / den) * p_g, 0.0)
                      + jnp.where(lane == i2, (e2 / den) * p_g, 0.0))


def _outproj(att, conv, x, gate1, shift2, scale2, ga, gc, woa, woc, g2, wrh, wrl, br, tm, tiles_per_group):
    n, d = x.shape
    rows = gate1.shape[1]
    full = lambda a: pl.BlockSpec(a.shape, lambda i: (0,) * a.ndim)
    row = lambda w: pl.BlockSpec((tm, w), lambda i: (i, 0))
    ms = _mod_spec(rows, d, tiles_per_group)
    return pl.pallas_call(
        _outproj_kernel,
        grid=(n // tm,),
        in_specs=[row(att.shape[1]), row(conv.shape[1]), row(d), ms, ms, ms,
                  full(ga), full(gc), full(woa), full(woc), full(g2), full(wrh), full(wrl), full(br)],
        out_specs=[row(d), row(d), row(LANES)],
        out_shape=[jax.ShapeDtypeStruct((n, d), F32), jax.ShapeDtypeStruct((n, d), BF16),
                   jax.ShapeDtypeStruct((n, LANES), F32)],
        compiler_params=_cparams(("parallel",)),
        name="outproj_router",
    )(att, conv, x, gate1, shift2, scale2, ga, gc, woa, woc, g2, wrh, wrl, br)


def _moe_kernel(h_ref, gates_ref, w1_ref, w3_ref, w2_ref, x1_ref, g2_ref, gf_ref, y_ref, acc_ref, *, f):
    j = pl.program_id(1)
    n_e = w1_ref.shape[1] // f

    @pl.when(j == 0)
    def _():
        acc_ref[...] = jnp.zeros(acc_ref.shape, F32)

    gates = gates_ref[...]
    lane = lax.broadcasted_iota(I32, gates.shape, 1)
    hb = h_ref[...]
    a = _dot(hb, w1_ref[...])
    b = _dot(hb, w3_ref[...])
    hid = (a * jax.nn.sigmoid(a)) * b
    gcols = [jnp.sum(jnp.where(lane == j * n_e + q, gates, 0.0), axis=-1, keepdims=True) for q in range(n_e)]
    hid = jnp.concatenate([hid[:, q * f:(q + 1) * f] * gcols[q] for q in range(n_e)], axis=1)
    acc_ref[...] += _dot(hid.astype(BF16), w2_ref[...])

    @pl.when(j == pl.num_programs(1) - 1)
    def _():
        xf = x1_ref[...] + g2_ref[0] * acc_ref[...]
        y_ref[...] = _rms(xf) * gf_ref[...]


def _moe(h2, gates, w1c, w3c, w2c, x1, gate2, g_final, tm, tiles_per_group, f):
    n, d = x1.shape
    wide = MOE_EXPERTS_PER_STEP * f
    rows = gate2.shape[1]
    row = lambda w: pl.BlockSpec((tm, w), lambda i, j: (i, 0))
    return pl.pallas_call(
        functools.partial(_moe_kernel, f=f),
        grid=(n // tm, w1c.shape[1] // wide),
        in_specs=[row(d), row(LANES),
                  pl.BlockSpec((d, wide), lambda i, j: (0, j)),
                  pl.BlockSpec((d, wide), lambda i, j: (0, j)),
                  pl.BlockSpec((wide, d), lambda i, j: (j, 0)),
                  row(d),
                  pl.BlockSpec((1, rows, d), lambda i, j: (i // tiles_per_group, 0, 0)),
                  pl.BlockSpec((1, d), lambda i, j: (0, 0))],
        out_specs=row(d),
        out_shape=jax.ShapeDtypeStruct((n, d), F32),
        scratch_shapes=[pltpu.VMEM((tm, d), F32)],
        compiler_params=_cparams(("parallel", "arbitrary")),
        name="moe_final",
    )(h2, gates, w1c, w3c, w2c, x1, gate2, g_final)


def _tile(n, pref):
    t = min(n, pref)
    assert n % t == 0, (n, pref)
    return t


def _prep_weights(w_in, w_out, w_rg, b_rg, w_re, b_re, w1, w3, w2):
    d = w_in.shape[0]
    off_qi = 3 * ATT_WIDTH
    off_ki = off_qi + IDX_HEADS * IDX_DIM
    off_conv = off_ki + IDX_DIM + IDX_HEADS
    cc = (w_in.shape[1] - off_conv) // 2
    wb = w_in.astype(BF16)
    wkw = jnp.zeros((d, LANES), BF16).at[:, :IDX_DIM + IDX_HEADS].set(wb[:, off_ki:off_conv])
    in_w = (wb[:, :off_qi], wb[:, off_qi:off_ki], wkw, wb[:, off_conv:off_conv + cc], wb[:, off_conv + cc:])
    wob = w_out.astype(BF16)
    out_w = (wob[:ATT_WIDTH], wob[ATT_WIDTH:])
    wr = jnp.zeros((d, LANES), F32)
    wr = wr.at[:, :N_EXPERTS].set(jnp.transpose(w_re, (1, 0, 2)).reshape(d, N_EXPERTS))
    wr = wr.at[:, N_EXPERTS:N_EXPERTS + N_GROUPS].set(w_rg)
    br = jnp.zeros((1, LANES), F32).at[0, :N_EXPERTS].set(b_re.reshape(-1))
    br = br.at[0, N_EXPERTS:N_EXPERTS + N_GROUPS].set(b_rg)
    wrh = wr.astype(BF16)
    wrl = (wr - wrh.astype(F32)).astype(BF16)
    n_e, _, f = w1.shape
    side_by_side = lambda w: jnp.transpose(w.astype(BF16), (1, 0, 2)).reshape(d, n_e * f)
    expert_w = (side_by_side(w1), side_by_side(w3), w2.astype(BF16).reshape(n_e * f, d), f)
    return in_w, out_w, (wrh, wrl, br), expert_w


def _token_layers(x2, mod_rows, tm, tm_moe, attn_fn, conv_fn, norms, weights, seq_len=None):
    g1, ga, gc, g2, gf = norms
    in_w, out_w, router_w, expert_w = weights
    (shift1, tpg), (scale1, _), (gate1, _), (shift2, _), (scale2, _) = [mod_rows(j, tm) for j in range(5)]
    gate2, tpg_moe = mod_rows(5, tm_moe)
    q, k, v, kb, vb, qi, kw, kib, glu = _inproj(x2, shift1, scale1, g1, in_w, tm, tpg, seq_len)
    att = attn_fn(q, k, v, kb, vb, qi, kw, kib)
    conv = conv_fn(glu)
    x1, h2, gates = _outproj(att, conv, x2, gate1, shift2, scale2, ga, gc, out_w[0], out_w[1], g2,
                             *router_w, tm, tpg)
    y = _moe(h2, gates, *expert_w[:3], x1, gate2, gf, tm_moe, tpg_moe, expert_w[3])
    return y, k, v, kw, glu


def kernel(x_prompt, x_sample, c_prompt, c_sample, cache_k, cache_v, cache_kidx, state_conv, page_table,
           w_ada, b_ada, g_norm1, w_in, conv_w, conv_b, conv_ln_g, conv_ln_b, g_attn_out, g_conv_out,
           w_out, g_norm2, w_rg, b_rg, w_re, b_re, w1, w3, w2, g_final):
    bp, s, d = x_prompt.shape
    bs, t_new, _ = x_sample.shape
    cc = conv_w.shape[1]
    n_pool, page = cache_k.shape[:2]
    hist = CONV_WIDTH - 1

    weights = _prep_weights(w_in, w_out, w_rg, b_rg, w_re, b_re, w1, w3, w2)
    norms = (g_norm1.reshape(1, d), g_attn_out.reshape(1, ATT_WIDTH), g_conv_out.reshape(1, cc),
             g_norm2.reshape(1, d), g_final.reshape(1, d))

    n_c = bp + bs
    pad = (-n_c) % 8
    c_all = jnp.concatenate([c_prompt, c_sample, jnp.zeros((pad, d), F32)], axis=0)
    mod = _modulation(c_all, w_ada, b_ada).reshape(n_c + pad, 6, d)

    tm_p = _tile(s, 256)

    def mod_p(j, tile):
        return mod[:bp, j].reshape(bp, 1, d), s // tile

    def prompt_attn(q, k, v, kb, vb, qi, kw, kib):
        r3 = lambda a: a.reshape(bp, s, a.shape[-1])
        tq = _tile(s, 512)
        tk = _tile(s, 512)
        return _prompt_attention(r3(q), r3(qi), r3(kw), r3(kb), r3(vb), r3(kib), tq, tk).reshape(bp * s, ATT_WIDTH)

    def prompt_conv(glu):
        prefix = jnp.zeros((bp, HALO, cc), F32)
        tt = _tile(s, 512)
        return _conv_module(glu.reshape(bp, s, cc), prefix, conv_w, conv_b, conv_ln_g, conv_ln_b, tt).reshape(bp * s, cc)

    y_p, k_p, v_p, kw_p, glu_p = _token_layers(
        x_prompt.reshape(bp * s, d), mod_p, tm_p, _tile(s, 512), prompt_attn, prompt_conv, norms, weights,
        seq_len=s)
    heads_last = lambda a: jnp.transpose(a.reshape(bp, N_HEADS, HEAD_DIM, s), (0, 3, 1, 2))
    k_p, v_p = heads_last(k_p), heads_last(v_p)

    kidx_t = jnp.transpose(cache_kidx, (0, 2, 1))
    k_t = jnp.transpose(cache_k, (0, 2, 3, 1)).reshape(n_pool, ATT_WIDTH, page)
    v_t = jnp.transpose(cache_v, (0, 2, 3, 1)).reshape(n_pool, ATT_WIDTH, page)
    n_s = bs * t_new
    tm_s = _tile(n_s, 256)
    mod_tok = jnp.repeat(mod[bp:bp + bs], t_new, axis=0)

    def mod_s(j, tile):
        return mod_tok[:, j].reshape(n_s // tile, tile, d), 1

    def sample_attn(q, k, v, kb, vb, qi, kw, kib):
        tp = 8 - t_new
        qi_rows = jnp.pad(qi.reshape(bs, t_new, IDX_HEADS, IDX_DIM).transpose(0, 2, 1, 3),
                          ((0, 0), (0, 0), (0, tp), (0, 0))).reshape(bs, 64, IDX_DIM)
        wi = kw[:, IDX_DIM:IDX_DIM + IDX_HEADS].reshape(bs, t_new, IDX_HEADS).transpose(0, 2, 1)
        w_rows = jnp.broadcast_to(jnp.pad(wi, ((0, 0), (0, 0), (0, tp))).reshape(bs, 64, 1), (bs, 64, LANES))
        rows_pad = lambda a: jnp.pad(a.reshape(bs, t_new, a.shape[-1]), ((0, 0), (0, LANES - t_new), (0, 0)))
        scores = _sample_scores(page_table, kidx_t, qi_rows, w_rows, rows_pad(kib), t_new)
        topk = min(TOPK_MAX, (page_table.shape[1] * page + t_new) // 4)
        bias = _sample_select(scores, topk, _tile(bs, SELECT_GROUP))
        qh = jnp.pad(q.reshape(bs, t_new, N_HEADS, HEAD_DIM).transpose(0, 2, 1, 3),
                     ((0, 0), (0, 0), (0, tp), (0, 0)))
        eye = jnp.eye(N_HEADS, dtype=q.dtype)
        q_bd = (qh[:, :, :, None, :] * eye[None, :, None, :, None]).reshape(bs, 64, ATT_WIDTH)
        o = _sample_attention(page_table, k_t, v_t, q_bd, bias, rows_pad(kb), rows_pad(vb))
        return o[:, :t_new].reshape(n_s, ATT_WIDTH)

    def sample_conv(glu):
        prefix = jnp.pad(state_conv, ((0, 0), (HALO - hist, 0), (0, 0)))
        glu8 = jnp.pad(glu.reshape(bs, t_new, cc), ((0, 0), (0, 8 - t_new), (0, 0)))
        buf = jnp.concatenate([prefix, glu8], axis=1)
        out = _conv_module_tail(buf, conv_w, conv_b, conv_ln_g, conv_ln_b, _tile(bs, SELECT_GROUP))
        return out[:, :t_new].reshape(n_s, cc)

    y_s, k_s, v_s, kw_s, glu_s = _token_layers(
        x_sample.reshape(n_s, d), mod_s, tm_s, _tile(n_s, 512), sample_attn, sample_conv, norms, weights)

    glu_p3 = glu_p.reshape(bp, s, cc)
    conv_prompt = glu_p3[:, s - hist:]
    conv_sample = jnp.concatenate([state_conv, glu_s.reshape(bs, t_new, cc)], axis=1)[:, -hist:]
    return (y_p.reshape(bp, s, d), y_s.reshape(bs, t_new, d),
            k_p, v_p,
            kw_p[:, :IDX_DIM].reshape(bp, s, IDX_DIM), conv_prompt,
            k_s.reshape(bs, t_new, N_HEADS, HEAD_DIM), v_s.reshape(bs, t_new, N_HEADS, HEAD_DIM),
            kw_s[:, :IDX_DIM].reshape(bs, t_new, IDX_DIM), conv_sample)
```

```python
import functools

import jax
import jax.numpy as jnp
from jax import lax
from jax.experimental import pallas as pl
from jax.experimental.pallas import tpu as pltpu

F32 = jnp.float32
BF16 = jnp.bfloat16
I32 = jnp.int32

EPS = 1e-6
HEAD_DIM = 64
N_HEADS = 8
ATT_WIDTH = N_HEADS * HEAD_DIM
IDX_HEADS = 8
IDX_DIM = 64
CONV_WIDTH = 31
TOPK_MAX = 256
N_GROUPS = 4
EXPERTS_PER_GROUP = 8
N_EXPERTS = N_GROUPS * EXPERTS_PER_GROUP
LANES = 128
HALO = 32
PAGES_PER_STEP = 32
SELECT_GROUP = 16
SELECT_ROWS = 128
PAIRS_PER_LOOP = 2
MOE_EXPERTS_PER_STEP = 8
NEG_BIG = -1e30
MAX_BISECT = 24
KEEP_ALL = 1e9
VMEM_LIMIT = 56 * 1024 * 1024


def _cparams(sem):
    return pltpu.CompilerParams(dimension_semantics=sem, vmem_limit_bytes=VMEM_LIMIT)


def _rms(x):
    return x * lax.rsqrt(jnp.mean(x * x, axis=-1, keepdims=True) + EPS)


def _dot(a, b):
    return jnp.dot(a, b, preferred_element_type=F32)


def _dot_nt(a, b):
    return lax.dot_general(a, b, (((1,), (1,)), ((), ())), preferred_element_type=F32)


def _mod_kernel(c_ref, w_ref, b_ref, o_ref):
    c = c_ref[...]
    a = (c * jax.nn.sigmoid(c)).astype(BF16)
    o_ref[...] = _dot(a, w_ref[...].astype(BF16)) + b_ref[...]


def _modulation(c, w_ada, b_ada, tn=512):
    r, d = c.shape
    n = w_ada.shape[1]
    return pl.pallas_call(
        _mod_kernel,
        grid=(n // tn,),
        in_specs=[pl.BlockSpec((r, d), lambda j: (0, 0)),
                  pl.BlockSpec((d, tn), lambda j: (0, j)),
                  pl.BlockSpec((1, tn), lambda j: (0, j))],
        out_specs=pl.BlockSpec((r, tn), lambda j: (0, j)),
        out_shape=jax.ShapeDtypeStruct((r, n), F32),
        compiler_params=_cparams(("arbitrary",)),
        name="adaln_mod",
    )(c, w_ada, b_ada.reshape(1, n))


def _inproj_kernel(x_ref, sh_ref, sc_ref, g_ref, wqkv_ref, wqi_ref, wkw_ref, wca_ref, wcb_ref,
                   q_ref, k_ref, v_ref, kb_ref, vb_ref, qi_ref, kw_ref, kib_ref, glu_ref, *, kv_seq_minor):
    h = _rms(x_ref[...]) * g_ref[...]
    h = h * (1.0 + sc_ref[0]) + sh_ref[0]
    hb = h.astype(BF16)
    qkv = _dot(hb, wqkv_ref[...])
    q_ref[...] = (qkv[:, :ATT_WIDTH] * (HEAD_DIM ** -0.5)).astype(BF16)
    k = qkv[:, ATT_WIDTH:2 * ATT_WIDTH]
    v = qkv[:, 2 * ATT_WIDTH:]
    if kv_seq_minor:
        k_ref[0] = k.T
        v_ref[0] = v.T
    else:
        k_ref[...] = k
        v_ref[...] = v
    kb_ref[...] = k.astype(BF16)
    vb_ref[...] = v.astype(BF16)
    qi_ref[...] = _dot(hb, wqi_ref[...]).astype(BF16)
    kw = _dot(hb, wkw_ref[...])
    kw_ref[...] = kw
    kib_ref[...] = kw[:, :IDX_DIM].astype(BF16)
    a = _dot(hb, wca_ref[...])
    b = _dot(hb, wcb_ref[...])
    glu_ref[...] = a * jax.nn.sigmoid(b)


def _mod_spec(rows, d, tiles_per_group):
    return pl.BlockSpec((1, rows, d), lambda i: (i // tiles_per_group, 0, 0))


def _inproj(x, shift, scale, g1, wts, tm, tiles_per_group, seq_len=None):
    n, d = x.shape
    wqkv, wqi, wkw, wca, wcb = wts
    cc = wca.shape[1]
    rows = shift.shape[1]
    full = lambda a: pl.BlockSpec(a.shape, lambda i: (0,) * a.ndim)
    row = lambda w: pl.BlockSpec((tm, w), lambda i: (i, 0))
    outs = [
        (ATT_WIDTH, BF16),
        (ATT_WIDTH, F32),
        (ATT_WIDTH, F32),
        (ATT_WIDTH, BF16),
        (ATT_WIDTH, BF16),
        (IDX_HEADS * IDX_DIM, BF16),
        (LANES, F32),
        (IDX_DIM, BF16),
        (cc, F32),
    ]
    out_specs = [row(w) for w, _ in outs]
    out_shape = [jax.ShapeDtypeStruct((n, w), dt) for w, dt in outs]
    if seq_len is not None:
        per_seq = seq_len // tm
        for o in (1, 2):
            out_specs[o] = pl.BlockSpec((1, ATT_WIDTH, tm), lambda i: (i // per_seq, 0, i % per_seq))
            out_shape[o] = jax.ShapeDtypeStruct((n // seq_len, ATT_WIDTH, seq_len), F32)
    return pl.pallas_call(
        functools.partial(_inproj_kernel, kv_seq_minor=seq_len is not None),
        grid=(n // tm,),
        in_specs=[row(d), _mod_spec(rows, d, tiles_per_group), _mod_spec(rows, d, tiles_per_group),
                  full(g1), full(wqkv), full(wqi), full(wkw), full(wca), full(wcb)],
        out_specs=out_specs,
        out_shape=out_shape,
        compiler_params=_cparams(("parallel",)),
        name="inproj",
    )(x, shift, scale, g1, wqkv, wqi, wkw, wca, wcb)


def _fold(scan, rows, init, fn, red):
    def body(sc, part, sl=slice(None)):
        for g in range(sc.shape[-1] // LANES):
            part = fn(part, sc[..., g * LANES:(g + 1) * LANES], sl)
        return part
    return red(scan(body, jnp.full(rows + (LANES,), init, F32)), axis=-1, keepdims=True)


def _kth_select(scan, rows, k):
    inf = jnp.inf
    count_ge = lambda t: _fold(scan, rows, 0.0, lambda p, x, sl: p + jnp.where(x >= t[sl], 1.0, 0.0), jnp.sum)
    count_gt = lambda t: _fold(scan, rows, 0.0, lambda p, x, sl: p + jnp.where(x > t[sl], 1.0, 0.0), jnp.sum)
    min_ge = lambda t: _fold(scan, rows, inf,
                             lambda p, x, sl: jnp.minimum(p, jnp.where(x >= t[sl], x, inf)), jnp.min)
    min_gt = lambda t: _fold(scan, rows, inf,
                             lambda p, x, sl: jnp.minimum(p, jnp.where(x > t[sl], x, inf)), jnp.min)
    row_max = _fold(scan, rows, -inf, lambda p, x, sl: jnp.maximum(p, x), jnp.max)
    row_min = min_gt(jnp.full(rows + (1,), -inf, F32))
    n_valid = count_ge(row_min)
    any_row = lambda flag: jnp.max(flag) > 0.5

    def bisect(lo, hi, cnt, pending):
        def cond(st):
            it, _, _, _, act = st
            return jnp.logical_and(it < MAX_BISECT, any_row(act))

        def body(st):
            it, lo, hi, cnt, act = st
            mid = 0.5 * (lo + hi)
            c = count_ge(mid)
            go_up = jnp.where(c >= k, act, 0.0) > 0.5
            go_dn = jnp.where(c >= k, 0.0, act) > 0.5
            lo2 = jnp.where(go_up, mid, lo)
            cnt2 = jnp.where(go_up, c, cnt)
            hi2 = jnp.where(go_dn, mid, hi)
            moving = jnp.logical_and(jnp.logical_and(mid > lo, mid < hi), cnt2 > k)
            return it + 1, lo2, hi2, cnt2, jnp.where(moving, act, 0.0)

        act0 = jnp.where(cnt > k, pending, 0.0)
        _, lo, hi, cnt, _ = lax.while_loop(cond, body, (jnp.int32(0), lo, hi, cnt, act0))
        return lo, hi, cnt

    def outer_cond(st):
        return any_row(st[5])

    def outer_body(st):
        lo, hi, cnt, thr, need, pending = st
        lo, hi, cnt = bisect(lo, hi, cnt, pending)
        plain = jnp.where(cnt <= k, pending, 0.0)
        hard = jnp.where(cnt <= k, 0.0, pending)
        thr = jnp.where(plain > 0.5, lo, thr)

        def resolve(_):
            t = min_ge(lo)
            n_gt = count_gt(t)
            tied = jnp.where(n_gt < k, hard, 0.0) > 0.5
            above_f = jnp.where(n_gt < k, 0.0, hard)
            above = above_f > 0.5
            return (jnp.where(above, min_gt(t), lo), jnp.where(above, n_gt, cnt),
                    jnp.where(tied, t, thr), jnp.where(tied, k - n_gt, need), above_f)

        def settled(_):
            return lo, cnt, thr, need, jnp.zeros(rows + (1,), F32)

        lo, cnt, thr, need, pending = lax.cond(any_row(hard), resolve, settled, 0)
        return lo, hi, cnt, thr, need, pending

    lo0 = row_min
    hi0 = row_max + (row_max - row_min) + 1.0
    st0 = (lo0, hi0, n_valid, lo0, jnp.full(rows + (1,), KEEP_ALL, F32), jnp.ones(rows + (1,), F32))
    _, _, _, thr, need, _ = lax.while_loop(outer_cond, outer_body, st0)
    return thr, need


def _prompt_attn_kernel(q_ref, qi_ref, kw_ref, kb_ref, vb_ref, kib_ref, o_ref,
                        score_ref, bias_ref, *, tq, tk, topk):
    i = pl.program_id(1)
    nk = ((i + 1) * tq + tk - 1) // tk
    lane = lax.broadcasted_iota(I32, (tq, tk), 1)
    qpos = i * tq + lax.broadcasted_iota(I32, (tq, tk), 0)
    wi = kw_ref[0][:, IDX_DIM:IDX_DIM + IDX_HEADS]
    qi = qi_ref[0]

    def score_chunk(c, carry):
        kic = kib_ref[0, pl.ds(pl.multiple_of(c * tk, tk), tk), :]
        acc = jnp.zeros((tq, tk), F32)
        for h in range(IDX_HEADS):
            s = _dot_nt(qi[:, h * IDX_DIM:(h + 1) * IDX_DIM], kic)
            acc = acc + wi[:, h:h + 1] * jnp.maximum(s, 0.0)
        valid = (c * tk + lane) <= qpos
        score_ref[c] = jnp.where(valid, acc, -jnp.inf)
        return carry
    lax.fori_loop(0, nk, score_chunk, 0)

    def scan(body, init):
        outs = []
        for r in range(0, tq, SELECT_ROWS):
            sl = slice(r, min(r + SELECT_ROWS, tq))
            outs.append(lax.fori_loop(0, nk, lambda c, part, sl=sl: body(score_ref[c, sl, :], part, sl),
                                      init[sl]))
        return jnp.concatenate(outs, axis=0)
    thr, need = _kth_select(scan, (tq,), topk)

    def bias_chunk(c, carry):
        bias_ref[c] = jnp.where(score_ref[c] >= thr, 0.0, NEG_BIG)
        return carry
    lax.fori_loop(0, nk, bias_chunk, 0)

    tie_f = jnp.where(need < KEEP_ALL, 1.0, 0.0)

    @pl.when(jnp.max(tie_f) > 0.5)
    def _():
        upper = (lax.broadcasted_iota(I32, (tk, tk), 0)
                 < lax.broadcasted_iota(I32, (tk, tk), 1)).astype(BF16)

        def tie_chunk(c, before):
            sc = score_ref[c]
            eq = sc == thr
            eqf = jnp.where(eq, 1.0, 0.0)
            rank = before + _dot(eqf.astype(BF16), upper)
            tie_bias = jnp.where(jnp.logical_or(sc > thr, jnp.logical_and(eq, rank < need)), 0.0, NEG_BIG)
            bias_ref[c] = jnp.where(tie_f > 0.5, tie_bias, bias_ref[c])
            return before + jnp.sum(eqf, axis=-1, keepdims=True)
        lax.fori_loop(0, nk, tie_chunk, jnp.zeros((tq, 1), F32))

    half = lax.broadcasted_iota(I32, (tq, LANES), 1) < HEAD_DIM
    for p0 in range(0, N_HEADS // 2, PAIRS_PER_LOOP):
        slabs = [slice(p * LANES, (p + 1) * LANES) for p in range(p0, p0 + PAIRS_PER_LOOP)]
        qs = []
        for sl in slabs:
            qp = q_ref[0, :, sl]
            qs.append((jnp.where(half, qp, jnp.zeros_like(qp)), jnp.where(half, jnp.zeros_like(qp), qp)))

        def attn_chunk(c, carry):
            off = pl.multiple_of(c * tk, tk)
            bias = bias_ref[c]
            new = []
            for n, sl in enumerate(slabs):
                kc = kb_ref[0, pl.ds(off, tk), sl]
                vc = vb_ref[0, pl.ds(off, tk), sl]
                for e in range(2):
                    m, l, acc = carry[2 * n + e]
                    s = _dot_nt(qs[n][e], kc) + bias
                    m_new = jnp.maximum(m, jnp.max(s, axis=-1, keepdims=True))
                    alpha = jnp.exp(m - m_new)
                    pr = jnp.exp(s - m_new)
                    l = alpha * l + jnp.sum(pr, axis=-1, keepdims=True)
                    acc = alpha * acc + _dot(pr.astype(BF16), vc)
                    new.append((m_new, l, acc))
            return tuple(new)

        init = tuple((jnp.full((tq, 1), NEG_BIG, F32), jnp.zeros((tq, 1), F32),
                      jnp.zeros((tq, LANES), F32)) for _ in range(2 * PAIRS_PER_LOOP))
        res = lax.fori_loop(0, nk, attn_chunk, init)
        for n, sl in enumerate(slabs):
            (_, l0, a0), (_, l1, a1) = res[2 * n], res[2 * n + 1]
            o_ref[0, :, sl] = jnp.where(half, a0 / l0, a1 / l1)


def _prompt_attention(q, qi, kw, kb, vb, kib, tq, tk):
    b, s, _ = q.shape
    topk = min(TOPK_MAX, s // 4)
    blk = lambda w: pl.BlockSpec((1, tq, w), lambda bi, i: (bi, i, 0))
    seq = lambda w: pl.BlockSpec((1, s, w), lambda bi, i: (bi, 0, 0))
    return pl.pallas_call(
        functools.partial(_prompt_attn_kernel, tq=tq, tk=tk, topk=topk),
        grid=(b, s // tq),
        in_specs=[blk(ATT_WIDTH), blk(IDX_HEADS * IDX_DIM), blk(LANES),
                  seq(ATT_WIDTH), seq(ATT_WIDTH), seq(IDX_DIM)],
        out_specs=blk(ATT_WIDTH),
        out_shape=jax.ShapeDtypeStruct((b, s, ATT_WIDTH), F32),
        scratch_shapes=[pltpu.VMEM((s // tk, tq, tk), F32), pltpu.VMEM((s // tk, tq, tk), F32)],
        compiler_params=_cparams(("parallel", "arbitrary")),
        name="prompt_dsa_attention",
    )(q, qi, kw, kb, vb, kib)


def _sample_scores_kernel(pt_ref, *refs, n_pages, t_new):
    del pt_ref
    pages = refs[:n_pages]
    qi_ref, w_ref, kin_ref, score_ref = refs[n_pages:]
    qi = qi_ref[0]
    w = w_ref[0]

    def combine(s):
        acc = jnp.zeros((8, LANES), F32)
        for h in range(IDX_HEADS):
            acc = acc + w[h * 8:(h + 1) * 8] * jnp.maximum(s[h * 8:(h + 1) * 8], 0.0)
        return acc

    for j in range(n_pages):
        page_t = pages[j][...].astype(BF16)
        score_ref[j] = combine(_dot(qi, page_t))

    s_new = combine(_dot_nt(qi, kin_ref[0]))
    lane = lax.broadcasted_iota(I32, (8, LANES), 1)
    row = lax.broadcasted_iota(I32, (8, LANES), 0)
    valid = jnp.logical_and(lane < t_new, lane <= row)
    score_ref[n_pages] = jnp.where(valid, s_new, -jnp.inf)


def _sample_scores(page_table, kidx_t, qi_rows, w_rows, ki_new, t_new):
    b, n_pages = page_table.shape
    page = kidx_t.shape[2]

    def page_spec(j):
        return pl.BlockSpec((None, IDX_DIM, page), lambda bi, pt: (pt[bi, j], 0, 0))
    per_seq = lambda shp: pl.BlockSpec((1,) + shp, lambda bi, pt: (bi,) + (0,) * len(shp))
    grid_spec = pltpu.PrefetchScalarGridSpec(
        num_scalar_prefetch=1,
        grid=(b,),
        in_specs=[page_spec(j) for j in range(n_pages)]
                 + [per_seq((64, IDX_DIM)), per_seq((64, LANES)), per_seq((LANES, IDX_DIM))],
        out_specs=pl.BlockSpec((n_pages + 1, None, 8, LANES), lambda bi, pt: (0, bi, 0, 0)),
    )
    return pl.pallas_call(
        functools.partial(_sample_scores_kernel, n_pages=n_pages, t_new=t_new),
        grid_spec=grid_spec,
        out_shape=jax.ShapeDtypeStruct((n_pages + 1, b, 8, LANES), F32),
        compiler_params=_cparams(("parallel",)),
        name="sample_dsa_scores",
    )(page_table, *([kidx_t] * n_pages), qi_rows, w_rows, ki_new)


def _sample_select_kernel(score_ref, bias_ref, *, topk):
    n_chunks, g = score_ref.shape[:2]
    unroll = max(u for u in range(1, 9) if n_chunks % u == 0)

    def scan(body, init):
        def step(c, part):
            for u in range(unroll):
                part = body(score_ref[c * unroll + u], part)
            return part
        return lax.fori_loop(0, n_chunks // unroll, step, init)
    thr, need = _kth_select(scan, (g, 8), topk)

    def bias_page(c, carry):
        bias_ref[c] = jnp.where(score_ref[c] >= thr, 0.0, NEG_BIG)
        return carry
    lax.fori_loop(0, n_chunks, bias_page, 0)

    tie_f = jnp.where(need < KEEP_ALL, 1.0, 0.0)

    @pl.when(jnp.max(tie_f) > 0.5)
    def _():
        upper = (lax.broadcasted_iota(I32, (LANES, LANES), 0)
                 < lax.broadcasted_iota(I32, (LANES, LANES), 1)).astype(BF16)

        def tie_page(c, before):
            sc = score_ref[c]
            eq = sc == thr
            eqf = jnp.where(eq, 1.0, 0.0)
            low = _dot(eqf.reshape(g * 8, LANES).astype(BF16), upper).reshape(g, 8, LANES)
            rank = before + low
            tie_bias = jnp.where(jnp.logical_or(sc > thr, jnp.logical_and(eq, rank < need)), 0.0, NEG_BIG)
            bias_ref[c] = jnp.where(tie_f > 0.5, tie_bias, bias_ref[c])
            return before + jnp.sum(eqf, axis=-1, keepdims=True)
        lax.fori_loop(0, n_chunks, tie_page, jnp.zeros((g, 8, 1), F32))


def _sample_select(scores, topk, group):
    n_chunks, b = scores.shape[:2]
    spec = pl.BlockSpec((n_chunks, group, 8, LANES), lambda i: (0, i, 0, 0))
    return pl.pallas_call(
        functools.partial(_sample_select_kernel, topk=topk),
        grid=(b // group,),
        in_specs=[spec],
        out_specs=spec,
        out_shape=jax.ShapeDtypeStruct(scores.shape, F32),
        compiler_params=_cparams(("parallel",)),
        name="sample_dsa_select",
    )(scores)


def _sample_attn_kernel(pt_ref, *refs, n_steps):
    del pt_ref
    kpages = refs[:PAGES_PER_STEP]
    vpages = refs[PAGES_PER_STEP:2 * PAGES_PER_STEP]
    (q_ref, bias_ref, bias_new_ref, kn_ref, vn_ref, o_ref,
     kt_ref, vt_ref, m_ref, l_ref, acc_ref) = refs[2 * PAGES_PER_STEP:]
    g = pl.program_id(1)
    q = q_ref[0]

    @pl.when(g == 0)
    def _():
        m_ref[...] = jnp.full(m_ref.shape, NEG_BIG, F32)
        l_ref[...] = jnp.zeros(l_ref.shape, F32)
        acc_ref[...] = jnp.zeros(acc_ref.shape, F32)

    def update(s, pv):
        m = m_ref[...]
        m_new = jnp.maximum(m, jnp.max(s, axis=-1, keepdims=True))
        alpha = jnp.exp(m - m_new)
        pr = jnp.exp(s - m_new)
        l_ref[...] = alpha * l_ref[...] + jnp.sum(pr, axis=-1, keepdims=True)
        acc_ref[...] = alpha * acc_ref[...] + pv(pr.astype(BF16))
        m_ref[...] = m_new

    tile8 = lambda b8: jnp.concatenate([b8] * N_HEADS, axis=0)

    for j in range(PAGES_PER_STEP):
        kt_ref[:, j * LANES:(j + 1) * LANES] = kpages[j][...].astype(BF16)
        vt_ref[:, j * LANES:(j + 1) * LANES] = vpages[j][...].astype(BF16)
    s = _dot(q, kt_ref[...])
    s = jnp.concatenate([s[:, j * LANES:(j + 1) * LANES] + tile8(bias_ref[j])
                         for j in range(PAGES_PER_STEP)], axis=1)
    update(s, lambda pr: _dot_nt(pr, vt_ref[...]))

    @pl.when(g == n_steps - 1)
    def _():
        update(_dot_nt(q, kn_ref[0]) + tile8(bias_new_ref[0]), lambda pr: _dot(pr, vn_ref[0]))
        o = acc_ref[...] / l_ref[...]
        lane = lax.broadcasted_iota(I32, (8, ATT_WIDTH), 1)
        out = jnp.zeros((8, ATT_WIDTH), F32)
        for h in range(N_HEADS):
            in_head = jnp.logical_and(lane >= h * HEAD_DIM, lane < (h + 1) * HEAD_DIM)
            out = out + jnp.where(in_head, o[h * 8:(h + 1) * 8], 0.0)
        o_ref[0] = out


def _sample_attention(page_table, k_t, v_t, q_bd, bias, k_new, v_new):
    b, n_pages = page_table.shape
    page = k_t.shape[2]
    n_steps = n_pages // PAGES_PER_STEP

    def page_spec(j):
        return pl.BlockSpec((None, ATT_WIDTH, page),
                            lambda bi, g, pt: (pt[bi, g * PAGES_PER_STEP + j], 0, 0))
    per_seq = lambda shp: pl.BlockSpec((1,) + shp, lambda bi, g, pt: (bi,) + (0,) * len(shp))
    grid_spec = pltpu.PrefetchScalarGridSpec(
        num_scalar_prefetch=1,
        grid=(b, n_steps),
        in_specs=[page_spec(j) for j in range(PAGES_PER_STEP)] * 2
                 + [per_seq((64, ATT_WIDTH)),
                    pl.BlockSpec((PAGES_PER_STEP, None, 8, LANES), lambda bi, g, pt: (g, bi, 0, 0)),
                    pl.BlockSpec((1, None, 8, LANES), lambda bi, g, pt: (n_pages, bi, 0, 0)),
                    per_seq((LANES, ATT_WIDTH)), per_seq((LANES, ATT_WIDTH))],
        out_specs=per_seq((8, ATT_WIDTH)),
        scratch_shapes=[pltpu.VMEM((ATT_WIDTH, PAGES_PER_STEP * page), BF16),
                        pltpu.VMEM((ATT_WIDTH, PAGES_PER_STEP * page), BF16),
                        pltpu.VMEM((64, 1), F32), pltpu.VMEM((64, 1), F32),
                        pltpu.VMEM((64, ATT_WIDTH), F32)],
    )
    return pl.pallas_call(
        functools.partial(_sample_attn_kernel, n_steps=n_steps),
        grid_spec=grid_spec,
        out_shape=jax.ShapeDtypeStruct((b, 8, ATT_WIDTH), F32),
        compiler_params=_cparams(("parallel", "arbitrary")),
        name="sample_dsa_attention",
    )(page_table, *([k_t] * PAGES_PER_STEP), *([v_t] * PAGES_PER_STEP),
      q_bd, bias, bias, k_new, v_new)


def _conv_kernel(cur_ref, prev_ref, pre_ref, cw_ref, cb_ref, g_ref, b_ref, o_ref, win_ref, *, tt):
    i = pl.program_id(1)

    @pl.when(i == 0)
    def _():
        win_ref[0:HALO] = pre_ref[0]

    if tt >= HALO:
        @pl.when(i > 0)
        def _():
            win_ref[0:HALO] = prev_ref[0, tt - HALO:tt, :]

    win_ref[HALO:HALO + tt] = cur_ref[0]
    acc = jnp.zeros(o_ref.shape[1:], F32) + cb_ref[...]
    for j in range(CONV_WIDTH):
        acc = acc + win_ref[pl.ds(j + HALO - (CONV_WIDTH - 1), tt), :] * cw_ref[j:j + 1, :]
    mu = jnp.mean(acc, axis=-1, keepdims=True)
    xc = acc - mu
    y = xc * lax.rsqrt(jnp.mean(xc * xc, axis=-1, keepdims=True) + EPS)
    y = y * g_ref[...] + b_ref[...]
    o_ref[0] = y * jax.nn.sigmoid(y)


def _conv_module(glu, prefix, conv_w, conv_b, ln_g, ln_b, tt):
    b, t, c = glu.shape
    assert tt >= HALO or t == tt, (t, tt)
    full = lambda a: pl.BlockSpec(a.shape, lambda bi, i: (0,) * a.ndim)
    cw = jnp.zeros((HALO, c), F32).at[:CONV_WIDTH].set(conv_w)
    return pl.pallas_call(
        functools.partial(_conv_kernel, tt=tt),
        grid=(b, t // tt),
        in_specs=[pl.BlockSpec((1, tt, c), lambda bi, i: (bi, i, 0)),
                  pl.BlockSpec((1, tt, c), lambda bi, i: (bi, jnp.maximum(i - 1, 0), 0)),
                  pl.BlockSpec((1, HALO, c), lambda bi, i: (bi, 0, 0)),
                  full(cw), pl.BlockSpec((1, c), lambda bi, i: (0, 0)),
                  pl.BlockSpec((1, c), lambda bi, i: (0, 0)), pl.BlockSpec((1, c), lambda bi, i: (0, 0))],
        out_specs=pl.BlockSpec((1, tt, c), lambda bi, i: (bi, i, 0)),
        out_shape=jax.ShapeDtypeStruct((b, t, c), F32),
        scratch_shapes=[pltpu.VMEM((HALO + tt, c), F32)],
        compiler_params=_cparams(("parallel", "arbitrary")),
        name="conv_module",
    )(glu, glu, prefix, cw, conv_b.reshape(1, c), ln_g.reshape(1, c), ln_b.reshape(1, c))


def _conv_tail_kernel(buf_ref, cw_ref, cb_ref, g_ref, b_ref, o_ref):
    acc = jnp.zeros(o_ref.shape, F32) + cb_ref[...]
    for j in range(CONV_WIDTH):
        acc = acc + buf_ref[:, pl.ds(j + HALO - (CONV_WIDTH - 1), 8), :] * cw_ref[j:j + 1, :]
    mu = jnp.mean(acc, axis=-1, keepdims=True)
    xc = acc - mu
    y = xc * lax.rsqrt(jnp.mean(xc * xc, axis=-1, keepdims=True) + EPS)
    y = y * g_ref[...] + b_ref[...]
    o_ref[...] = y * jax.nn.sigmoid(y)


def _conv_module_tail(buf, conv_w, conv_b, ln_g, ln_b, group):
    b, rows, c = buf.shape
    cw = jnp.zeros((HALO, c), F32).at[:CONV_WIDTH].set(conv_w)
    vec = pl.BlockSpec((1, c), lambda i: (0, 0))
    return pl.pallas_call(
        _conv_tail_kernel,
        grid=(b // group,),
        in_specs=[pl.BlockSpec((group, rows, c), lambda i: (i, 0, 0)),
                  pl.BlockSpec((HALO, c), lambda i: (0, 0)), vec, vec, vec],
        out_specs=pl.BlockSpec((group, 8, c), lambda i: (i, 0, 0)),
        out_shape=jax.ShapeDtypeStruct((b, 8, c), F32),
        compiler_params=_cparams(("parallel",)),
        name="conv_module_tail",
    )(buf, cw, conv_b.reshape(1, c), ln_g.reshape(1, c), ln_b.reshape(1, c))


def _outproj_kernel(att_ref, conv_ref, x_ref, g1_ref, sh_ref, sc_ref, ga_ref, gc_ref, woa_ref, woc_ref,
                    g2_ref, wrh_ref, wrl_ref, br_ref, x1_ref, h2_ref, gates_ref):
    a = (_rms(att_ref[...]) * ga_ref[...]).astype(BF16)
    c = (_rms(conv_ref[...]) * gc_ref[...]).astype(BF16)
    mixed = _dot(a, woa_ref[...]) + _dot(c, woc_ref[...])
    x1 = x_ref[...] + g1_ref[0] * mixed
    x1_ref[...] = x1
    h2 = (_rms(x1) * g2_ref[...]) * (1.0 + sc_ref[0]) + sh_ref[0]
    hi = h2.astype(BF16)
    h2_ref[...] = hi
    lo = (h2 - hi.astype(F32)).astype(BF16)
    logits = _dot(hi, wrh_ref[...]) + _dot(lo, wrh_ref[...]) + _dot(hi, wrl_ref[...]) + br_ref[...]
    lane = lax.broadcasted_iota(I32, logits.shape, 1).astype(F32)
    neg_inf = -jnp.inf
    far = float(4 * LANES)
    is_g = jnp.logical_and(lane >= N_EXPERTS, lane < N_EXPERTS + N_GROUPS)
    gl = jnp.where(is_g, logits, neg_inf)
    gmax = jnp.max(gl, axis=-1, keepdims=True)
    gsel = jnp.min(jnp.where(gl == gmax, lane, far), axis=-1, keepdims=True) - N_EXPERTS
    p_g = 1.0 / jnp.sum(jnp.exp(gl - gmax), axis=-1, keepdims=True)
    in_grp = jnp.logical_and(lane >= gsel * EXPERTS_PER_GROUP, lane < (gsel + 1.0) * EXPERTS_PER_GROUP)
    el = jnp.where(in_grp, logits, neg_inf)
    v1 = jnp.max(el, axis=-1, keepdims=True)
    i1 = jnp.min(jnp.where(el == v1, lane, far), axis=-1, keepdims=True)
    el2 = jnp.where(lane == i1, neg_inf, el)
    v2 = jnp.max(el2, axis=-1, keepdims=True)
    i2 = jnp.min(jnp.where(el2 == v2, lane, far), axis=-1, keepdims=True)
    e2 = jnp.exp(v2 - v1)
    den = 1.0 + e2
    gates_ref[...] = (jnp.where(lane == i1, (1.0 / den) * p_g, 0.0)
                      + jnp.where(lane == i2, (e2 / den) * p_g, 0.0))


def _outproj(att, conv, x, gate1, shift2, scale2, ga, gc, woa, woc, g2, wrh, wrl, br, tm, tiles_per_group):
    n, d = x.shape
    rows = gate1.shape[1]
    full = lambda a: pl.BlockSpec(a.shape, lambda i: (0,) * a.ndim)
    row = lambda w: pl.BlockSpec((tm, w), lambda i: (i, 0))
    ms = _mod_spec(rows, d, tiles_per_group)
    return pl.pallas_call(
        _outproj_kernel,
        grid=(n // tm,),
        in_specs=[row(att.shape[1]), row(conv.shape[1]), row(d), ms, ms, ms,
                  full(ga), full(gc), full(woa), full(woc), full(g2), full(wrh), full(wrl), full(br)],
        out_specs=[row(d), row(d), row(LANES)],
        out_shape=[jax.ShapeDtypeStruct((n, d), F32), jax.ShapeDtypeStruct((n, d), BF16),
                   jax.ShapeDtypeStruct((n, LANES), F32)],
        compiler_params=_cparams(("parallel",)),
        name="outproj_router",
    )(att, conv, x, gate1, shift2, scale2, ga, gc, woa, woc, g2, wrh, wrl, br)


def _moe_kernel(h_ref, gates_ref, w1_ref, w3_ref, w2_ref, x1_ref, g2_ref, gf_ref, y_ref, acc_ref, *, f):
    j = pl.program_id(1)
    n_e = w1_ref.shape[1] // f

    @pl.when(j == 0)
    def _():
        acc_ref[...] = jnp.zeros(acc_ref.shape, F32)

    gates = gates_ref[...]
    lane = lax.broadcasted_iota(I32, gates.shape, 1)
    hb = h_ref[...]
    a = _dot(hb, w1_ref[...])
    b = _dot(hb, w3_ref[...])
    hid = (a * jax.nn.sigmoid(a)) * b
    gcols = [jnp.sum(jnp.where(lane == j * n_e + q, gates, 0.0), axis=-1, keepdims=True) for q in range(n_e)]
    hid = jnp.concatenate([hid[:, q * f:(q + 1) * f] * gcols[q] for q in range(n_e)], axis=1)
    acc_ref[...] += _dot(hid.astype(BF16), w2_ref[...])

    @pl.when(j == pl.num_programs(1) - 1)
    def _():
        xf = x1_ref[...] + g2_ref[0] * acc_ref[...]
        y_ref[...] = _rms(xf) * gf_ref[...]


def _moe(h2, gates, w1c, w3c, w2c, x1, gate2, g_final, tm, tiles_per_group, f):
    n, d = x1.shape
    wide = MOE_EXPERTS_PER_STEP * f
    rows = gate2.shape[1]
    row = lambda w: pl.BlockSpec((tm, w), lambda i, j: (i, 0))
    return pl.pallas_call(
        functools.partial(_moe_kernel, f=f),
        grid=(n // tm, w1c.shape[1] // wide),
        in_specs=[row(d), row(LANES),
                  pl.BlockSpec((d, wide), lambda i, j: (0, j)),
                  pl.BlockSpec((d, wide), lambda i, j: (0, j)),
                  pl.BlockSpec((wide, d), lambda i, j: (j, 0)),
                  row(d),
                  pl.BlockSpec((1, rows, d), lambda i, j: (i // tiles_per_group, 0, 0)),
                  pl.BlockSpec((1, d), lambda i, j: (0, 0))],
        out_specs=row(d),
        out_shape=jax.ShapeDtypeStruct((n, d), F32),
        scratch_shapes=[pltpu.VMEM((tm, d), F32)],
        compiler_params=_cparams(("parallel", "arbitrary")),
        name="moe_final",
    )(h2, gates, w1c, w3c, w2c, x1, gate2, g_final)


def _tile(n, pref):
    t = min(n, pref)
    assert n % t == 0, (n, pref)
    return t


def _prep_weights(w_in, w_out, w_rg, b_rg, w_re, b_re, w1, w3, w2):
    d = w_in.shape[0]
    off_qi = 3 * ATT_WIDTH
    off_ki = off_qi + IDX_HEADS * IDX_DIM
    off_conv = off_ki + IDX_DIM + IDX_HEADS
    cc = (w_in.shape[1] - off_conv) // 2
    wb = w_in.astype(BF16)
    wkw = jnp.zeros((d, LANES), BF16).at[:, :IDX_DIM + IDX_HEADS].set(wb[:, off_ki:off_conv])
    in_w = (wb[:, :off_qi], wb[:, off_qi:off_ki], wkw, wb[:, off_conv:off_conv + cc], wb[:, off_conv + cc:])
    wob = w_out.astype(BF16)
    out_w = (wob[:ATT_WIDTH], wob[ATT_WIDTH:])
    wr = jnp.zeros((d, LANES), F32)
    wr = wr.at[:, :N_EXPERTS].set(jnp.transpose(w_re, (1, 0, 2)).reshape(d, N_EXPERTS))
    wr = wr.at[:, N_EXPERTS:N_EXPERTS + N_GROUPS].set(w_rg)
    br = jnp.zeros((1, LANES), F32).at[0, :N_EXPERTS].set(b_re.reshape(-1))
    br = br.at[0, N_EXPERTS:N_EXPERTS + N_GROUPS].set(b_rg)
    wrh = wr.astype(BF16)
    wrl = (wr - wrh.astype(F32)).astype(BF16)
    n_e, _, f = w1.shape
    side_by_side = lambda w: jnp.transpose(w.astype(BF16), (1, 0, 2)).reshape(d, n_e * f)
    expert_w = (side_by_side(w1), side_by_side(w3), w2.astype(BF16).reshape(n_e * f, d), f)
    return in_w, out_w, (wrh, wrl, br), expert_w


def _token_layers(x2, mod_rows, tm, tm_moe, attn_fn, conv_fn, norms, weights, seq_len=None):
    g1, ga, gc, g2, gf = norms
    in_w, out_w, router_w, expert_w = weights
    (shift1, tpg), (scale1, _), (gate1, _), (shift2, _), (scale2, _) = [mod_rows(j, tm) for j in range(5)]
    gate2, tpg_moe = mod_rows(5, tm_moe)
    q, k, v, kb, vb, qi, kw, kib, glu = _inproj(x2, shift1, scale1, g1, in_w, tm, tpg, seq_len)
    att = attn_fn(q, k, v, kb, vb, qi, kw, kib)
    conv = conv_fn(glu)
    x1, h2, gates = _outproj(att, conv, x2, gate1, shift2, scale2, ga, gc, out_w[0], out_w[1], g2,
                             *router_w, tm, tpg)
    y = _moe(h2, gates, *expert_w[:3], x1, gate2, gf, tm_moe, tpg_moe, expert_w[3])
    return y, k, v, kw, glu


def kernel(x_prompt, x_sample, c_prompt, c_sample, cache_k, cache_v, cache_kidx, state_conv, page_table,
           w_ada, b_ada, g_norm1, w_in, conv_w, conv_b, conv_ln_g, conv_ln_b, g_attn_out, g_conv_out,
           w_out, g_norm2, w_rg, b_rg, w_re, b_re, w1, w3, w2, g_final):
    bp, s, d = x_prompt.shape
    bs, t_new, _ = x_sample.shape
    cc = conv_w.shape[1]
    n_pool, page = cache_k.shape[:2]
    hist = CONV_WIDTH - 1

    weights = _prep_weights(w_in, w_out, w_rg, b_rg, w_re, b_re, w1, w3, w2)
    norms = (g_norm1.reshape(1, d), g_attn_out.reshape(1, ATT_WIDTH), g_conv_out.reshape(1, cc),
             g_norm2.reshape(1, d), g_final.reshape(1, d))

    n_c = bp + bs
    pad = (-n_c) % 8
    c_all = jnp.concatenate([c_prompt, c_sample, jnp.zeros((pad, d), F32)], axis=0)
    mod = _modulation(c_all, w_ada, b_ada).reshape(n_c + pad, 6, d)

    tm_p = _tile(s, 256)

    def mod_p(j, tile):
        return mod[:bp, j].reshape(bp, 1, d), s // tile

    def prompt_attn(q, k, v, kb, vb, qi, kw, kib):
        r3 = lambda a: a.reshape(bp, s, a.shape[-1])
        tq = _tile(s, 512)
        tk = _tile(s, 512)
        return _prompt_attention(r3(q), r3(qi), r3(kw), r3(kb), r3(vb), r3(kib), tq, tk).reshape(bp * s, ATT_WIDTH)

    def prompt_conv(glu):
        prefix = jnp.zeros((bp, HALO, cc), F32)
        tt = _tile(s, 512)
        return _conv_module(glu.reshape(bp, s, cc), prefix, conv_w, conv_b, conv_ln_g, conv_ln_b, tt).reshape(bp * s, cc)

    y_p, k_p, v_p, kw_p, glu_p = _token_layers(
        x_prompt.reshape(bp * s, d), mod_p, tm_p, _tile(s, 512), prompt_attn, prompt_conv, norms, weights,
        seq_len=s)
    heads_last = lambda a: jnp.transpose(a.reshape(bp, N_HEADS, HEAD_DIM, s), (0, 3, 1, 2))
    k_p, v_p = heads_last(k_p), heads_last(v_p)

    kidx_t = jnp.transpose(cache_kidx, (0, 2, 1))
    k_t = jnp.transpose(cache_k, (0, 2, 3, 1)).reshape(n_pool, ATT_WIDTH, page)
    v_t = jnp.transpose(cache_v, (0, 2, 3, 1)).reshape(n_pool, ATT_WIDTH, page)
    n_s = bs * t_new
    tm_s = _tile(n_s, 256)
    mod_tok = jnp.repeat(mod[bp:bp + bs], t_new, axis=0)

    def mod_s(j, tile):
        return mod_tok[:, j].reshape(n_s // tile, tile, d), 1

    def sample_attn(q, k, v, kb, vb, qi, kw, kib):
        tp = 8 - t_new
        qi_rows = jnp.pad(qi.reshape(bs, t_new, IDX_HEADS, IDX_DIM).transpose(0, 2, 1, 3),
                          ((0, 0), (0, 0), (0, tp), (0, 0))).reshape(bs, 64, IDX_DIM)
        wi = kw[:, IDX_DIM:IDX_DIM + IDX_HEADS].reshape(bs, t_new, IDX_HEADS).transpose(0, 2, 1)
        w_rows = jnp.broadcast_to(jnp.pad(wi, ((0, 0), (0, 0), (0, tp))).reshape(bs, 64, 1), (bs, 64, LANES))
        rows_pad = lambda a: jnp.pad(a.reshape(bs, t_new, a.shape[-1]), ((0, 0), (0, LANES - t_new), (0, 0)))
        scores = _sample_scores(page_table, kidx_t, qi_rows, w_rows, rows_pad(kib), t_new)
        topk = min(TOPK_MAX, (page_table.shape[1] * page + t_new) // 4)
        bias = _sample_select(scores, topk, _tile(bs, SELECT_GROUP))
        qh = jnp.pad(q.reshape(bs, t_new, N_HEADS, HEAD_DIM).transpose(0, 2, 1, 3),
                     ((0, 0), (0, 0), (0, tp), (0, 0)))
        eye = jnp.eye(N_HEADS, dtype=q.dtype)
        q_bd = (qh[:, :, :, None, :] * eye[None, :, None, :, None]).reshape(bs, 64, ATT_WIDTH)
        o = _sample_attention(page_table, k_t, v_t, q_bd, bias, rows_pad(kb), rows_pad(vb))
        return o[:, :t_new].reshape(n_s, ATT_WIDTH)

    def sample_conv(glu):
        prefix = jnp.pad(state_conv, ((0, 0), (HALO - hist, 0), (0, 0)))
        glu8 = jnp.pad(glu.reshape(bs, t_new, cc), ((0, 0), (0, 8 - t_new), (0, 0)))
        buf = jnp.concatenate([prefix, glu8], axis=1)
        out = _conv_module_tail(buf, conv_w, conv_b, conv_ln_g, conv_ln_b, _tile(bs, SELECT_GROUP))
        return out[:, :t_new].reshape(n_s, cc)

    y_s, k_s, v_s, kw_s, glu_s = _token_layers(
        x_sample.reshape(n_s, d), mod_s, tm_s, _tile(n_s, 512), sample_attn, sample_conv, norms, weights)

    glu_p3 = glu_p.reshape(bp, s, cc)
    conv_prompt = glu_p3[:, s - hist:]
    conv_sample = jnp.concatenate([state_conv, glu_s.reshape(bs, t_new, cc)], axis=1)[:, -hist:]
    return (y_p.reshape(bp, s, d), y_s.reshape(bs, t_new, d),
            k_p, v_p,
            kw_p[:, :IDX_DIM].reshape(bp, s, IDX_DIM), conv_prompt,
            k_s.reshape(bs, t_new, N_HEADS, HEAD_DIM), v_s.reshape(bs, t_new, N_HEADS, HEAD_DIM),
            kw_s[:, :IDX_DIM].reshape(bs, t_new, IDX_DIM), conv_sample)
```
